```python
import math
import jax, jax.numpy as jnp
from jax import lax
import numpy as np


D_MODEL = 1024
BATCH = 2
SEQ = 8192
DEPTH = 1
DEC_BATCH = 16
DEC_SEQ = 16
PAST_LEN = 4096

CHUNK = 64
Q_BLOCK = 128
A_HEADS = 4
A_QK_DIM = 64
A_V_DIM = 128
A_WIDTH = A_HEADS * A_V_DIM
R_HEADS = 4
R_K_DIM = 128
R_V_DIM = 128
R_WIDTH = R_HEADS * R_V_DIM
D_FF = 2816
CONV_W = 3
N_BUCKETS = 32
MAX_DISTANCE = 128
ROPE_BASE = 10000.0
EPS = 1e-6
NEG_INF = -1e30

kernel_name = 'hymba_diffattn_retnet_convffn_adaln_step'


def rmsnorm(x, g):
    xf = x.astype(jnp.float32)
    y = xf * lax.rsqrt(jnp.mean(xf * xf, axis=-1, keepdims=True) + EPS)
    return (y * g.astype(jnp.float32)).astype(x.dtype)


def adaln(c, w_ada, b_ada):
    m = jax.nn.silu(c) @ w_ada + b_ada
    return [t[:, None, :] for t in jnp.split(m, 6, axis=-1)]


def rotary(x, pos):
    half = x.shape[-1] // 2
    inv_freq = ROPE_BASE ** (-jnp.arange(half, dtype=jnp.float32) / half)
    ang = pos.astype(jnp.float32)[:, None] * inv_freq[None, :]
    cos = jnp.cos(ang)[None, :, None, :]
    sin = jnp.sin(ang)[None, :, None, :]
    x1 = x[..., :half].astype(jnp.float32)
    x2 = x[..., half:].astype(jnp.float32)
    return jnp.concatenate([x1 * cos - x2 * sin, x1 * sin + x2 * cos], axis=-1).astype(x.dtype)


def t5_bucket(rel):
    half = N_BUCKETS // 2
    max_exact = half // 2
    ret = jnp.where(rel > 0, half, 0)
    n = jnp.abs(rel)
    large = max_exact + (jnp.log(jnp.maximum(n, 1).astype(jnp.float32) / max_exact)
                         / math.log(MAX_DISTANCE / max_exact) * (half - max_exact)).astype(jnp.int32)
    large = jnp.minimum(large, half - 1)
    return ret + jnp.where(n < max_exact, n, large)


def diff_attention_block(q, k, v, q_pos, k_pos, rel_bias, lam):
    bucket = t5_bucket(k_pos[None, :] - q_pos[:, None])
    bias = jnp.transpose(rel_bias[bucket].astype(jnp.float32), (2, 0, 1))[None]
    visible = (k_pos[None, :] // CHUNK) <= (q_pos[:, None] // CHUNK)
    scale = A_QK_DIM ** -0.5

    def probs(qi, ki):
        s = jnp.einsum('bqhd,bkhd->bhqk', qi, ki).astype(jnp.float32) * scale + bias
        return jax.nn.softmax(jnp.where(visible, s, NEG_INF), axis=-1)

    p = (probs(q[..., :A_QK_DIM], k[..., :A_QK_DIM])
         - lam * probs(q[..., A_QK_DIM:], k[..., A_QK_DIM:]))
    return jnp.einsum('bhqk,bkhd->bqhd', p.astype(v.dtype), v)


def diff_attention_prompt(q, k, v, rel_bias, lam):
    b, s, h, dq = q.shape
    nb = s // Q_BLOCK
    k_pos = jnp.arange(s, dtype=jnp.int32)
    q_blocks = q.reshape(b, nb, Q_BLOCK, h, dq).swapaxes(0, 1)
    starts = jnp.arange(nb, dtype=jnp.int32) * Q_BLOCK

    def one_block(args):
        qb, s0 = args
        return diff_attention_block(qb, k, v, s0 + jnp.arange(Q_BLOCK, dtype=jnp.int32),
                                    k_pos, rel_bias, lam)

    o = lax.map(one_block, (q_blocks, starts))
    return o.swapaxes(0, 1).reshape(b, s, h, v.shape[-1])


def retention_chunk(state, q, k, v, log_gamma):
    q, k, v = (t.astype(jnp.float32) for t in (q, k, v))
    L = q.shape[1]
    idx = jnp.arange(L, dtype=jnp.float32)
    diff = idx[:, None] - idx[None, :]
    decay = jnp.where(diff >= 0, jnp.exp(log_gamma[:, None, None] * jnp.maximum(diff, 0.0)), 0.0)
    scores = jnp.einsum('bihd,bjhd->bhij', q, k) * decay
    o = jnp.einsum('bhij,bjhe->bihe', scores, v)
    q_decay = jnp.exp(log_gamma[None, :] * (idx[:, None] + 1.0))
    o = o + jnp.einsum('bihd,bhde->bihe', q, state) * q_decay[None, :, :, None]
    k_decay = jnp.exp(log_gamma[None, :] * (L - 1.0 - idx[:, None]))
    new_state = (jnp.exp(log_gamma * L)[None, :, None, None] * state
                 + jnp.einsum('bjhd,bjhe->bhde', k * k_decay[None, :, :, None], v))
    return o, new_state


def retention_prompt(q, k, v, log_gamma):
    b, s, h, dk = q.shape
    n = s // CHUNK

    def to_chunks(t):
        return t.reshape(b, n, CHUNK, h, t.shape[-1]).swapaxes(0, 1)

    s0 = jnp.zeros((b, h, dk, v.shape[-1]), jnp.float32)

    def step(state, qkv):
        o, state = retention_chunk(state, qkv[0], qkv[1], qkv[2], log_gamma)
        return state, o

    s_final, o = lax.scan(step, s0, (to_chunks(q), to_chunks(k), to_chunks(v)))
    return o.swapaxes(0, 1).reshape(b, s, h, v.shape[-1]), s_final


def split_projection(h, w_in, pos):
    b, L, _ = h.shape
    sizes = [A_HEADS * 2 * A_QK_DIM, A_HEADS * 2 * A_QK_DIM, A_WIDTH,
             R_HEADS * R_K_DIM, R_HEADS * R_K_DIM, R_WIDTH, R_WIDTH]
    offsets = [int(o) for o in np.cumsum(sizes)[:-1]]
    qa, ka, va, qr, kr, vr, gr = jnp.split(h @ w_in, offsets, axis=-1)

    def heads(t, n):
        return t.reshape(b, L, n, t.shape[-1] // n)

    qr = rotary(heads(qr, R_HEADS), pos)
    kr = rotary(heads(kr, R_HEADS), pos) * (R_K_DIM ** -0.5)
    return (heads(qa, A_HEADS), heads(ka, A_HEADS), heads(va, A_HEADS),
            qr, kr, heads(vr, R_HEADS), gr)


def conv_ffn(h, conv_prev, w_up, w_conv, b_conv, w_down):
    u = h @ w_up
    L = u.shape[1]
    ue = jnp.concatenate([conv_prev.astype(u.dtype), u], axis=1)
    y = sum(w_conv[j] * ue[:, j:j + L] for j in range(CONV_W)) + b_conv
    a, g = jnp.split(y, 2, axis=-1)
    return (jax.nn.silu(a) * g) @ w_down, ue[:, L:]


def layer(x, c, pos, conv_prev, attend, retain, w_ada, b_ada, g_mix, w_in, lam_init,
          g_sub_a, g_sub_r, w_out, g_ffn, w_up, w_conv, b_conv, w_down):
    sh_m, sc_m, gt_m, sh_f, sc_f, gt_f = adaln(c, w_ada, b_ada)
    b, L, _ = x.shape
    h = rmsnorm(x, g_mix) * (1.0 + sc_m) + sh_m
    qa, ka, va, qr, kr, vr, gr = split_projection(h, w_in, pos)
    oa = attend(qa, ka, va)
    orr, ret_state = retain(qr, kr, vr)
    oa = rmsnorm(oa, g_sub_a) * (1.0 - lam_init)
    orr = rmsnorm(orr.astype(x.dtype), g_sub_r)
    mixed = jnp.concatenate([oa.reshape(b, L, A_WIDTH),
                             jax.nn.silu(gr) * orr.reshape(b, L, R_WIDTH)], axis=-1)
    x = x + gt_m * (mixed @ w_out)
    h = rmsnorm(x, g_ffn) * (1.0 + sc_f) + sh_f
    f, conv_state = conv_ffn(h, conv_prev, w_up, w_conv, b_conv, w_down)
    x = x + gt_f * f
    return x, ka, va, ret_state, conv_state


def setup_inputs(seed: int = 0) -> dict:
    key = jax.random.key(seed)
    ks = jax.random.split(key, 27)

    def nrm(k, shape, scale):
        return jax.random.normal(k, shape, jnp.float32) * scale

    mix_w = A_WIDTH + R_WIDTH
    in_cols = 2 * A_HEADS * 2 * A_QK_DIM + A_WIDTH + 2 * R_HEADS * R_K_DIM + 2 * R_WIDTH
    return {
        'x_prompt': nrm(ks[0], (BATCH, SEQ, D_MODEL), 1.0),
        'x_sample': nrm(ks[1], (DEC_BATCH, DEC_SEQ, D_MODEL), 1.0),
        'cache_k': nrm(ks[2], (DEPTH, DEC_BATCH, PAST_LEN, A_HEADS, 2 * A_QK_DIM), 1.0),
        'cache_v': nrm(ks[3], (DEPTH, DEC_BATCH, PAST_LEN, A_HEADS, A_V_DIM), 1.0),
        'state_ret': nrm(ks[4], (DEPTH, DEC_BATCH, R_HEADS, R_K_DIM, R_V_DIM), 0.1),
        'state_conv': nrm(ks[5], (DEPTH, DEC_BATCH, CONV_W - 1, 2 * D_FF), 1.0),
        'c_prompt': nrm(ks[6], (BATCH, D_MODEL), 1.0),
        'c_sample': nrm(ks[7], (DEC_BATCH, D_MODEL), 1.0),
        'w_ada': nrm(ks[8], (DEPTH, D_MODEL, 6 * D_MODEL), 0.5 * D_MODEL ** -0.5),
        'b_ada': nrm(ks[9], (DEPTH, 6 * D_MODEL), 0.01),
        'g_mix': 1.0 + nrm(ks[10], (DEPTH, D_MODEL), 0.01),
        'w_in': nrm(ks[11], (DEPTH, D_MODEL, in_cols), D_MODEL ** -0.5),
        'lambda_q1': nrm(ks[12], (DEPTH, A_QK_DIM), 0.1),
        'lambda_k1': nrm(ks[13], (DEPTH, A_QK_DIM), 0.1),
        'lambda_q2': nrm(ks[14], (DEPTH, A_QK_DIM), 0.1),
        'lambda_k2': nrm(ks[15], (DEPTH, A_QK_DIM), 0.1),
        'g_sub_a': 1.0 + nrm(ks[16], (DEPTH, A_V_DIM), 0.01),
        'g_sub_r': 1.0 + nrm(ks[17], (DEPTH, R_V_DIM), 0.01),
        'w_out': nrm(ks[18], (DEPTH, mix_w, D_MODEL), mix_w ** -0.5),
        'g_ffn': 1.0 + nrm(ks[19], (DEPTH, D_MODEL), 0.01),
        'w_up': nrm(ks[20], (DEPTH, D_MODEL, 2 * D_FF), D_MODEL ** -0.5),
        'w_conv': nrm(ks[21], (DEPTH, CONV_W, 2 * D_FF), CONV_W ** -0.5),
        'b_conv': nrm(ks[22], (DEPTH, 2 * D_FF), 0.01),
        'w_down': nrm(ks[23], (DEPTH, D_FF, D_MODEL), D_FF ** -0.5),
        'rel_bias': nrm(ks[24], (N_BUCKETS, A_HEADS), 0.1),
        'g_final': 1.0 + nrm(ks[25], (D_MODEL,), 0.01),
    }


def reference(x_prompt, x_sample, cache_k, cache_v, state_ret, state_conv, c_prompt, c_sample,
              w_ada, b_ada, g_mix, w_in, lambda_q1, lambda_k1, lambda_q2, lambda_k2,
              g_sub_a, g_sub_r, w_out, g_ffn, w_up, w_conv, b_conv, w_down, rel_bias, g_final):
    f32 = jnp.float32
    log_gamma = jnp.log(1.0 - 2.0 ** (-5.0 - jnp.arange(R_HEADS, dtype=f32)))
    s_prompt = x_prompt.shape[1]
    s_new = x_sample.shape[1]
    past = cache_k.shape[2]
    pos_p = jnp.arange(s_prompt, dtype=jnp.int32)
    pos_s = past + jnp.arange(s_new, dtype=jnp.int32)
    k_pos_s = jnp.arange(past + s_new, dtype=jnp.int32)

    xp, xs = x_prompt, x_sample
    kp, vp, rp, cp, ksl, vsl, rsl, csl = [], [], [], [], [], [], [], []
    for l in range(DEPTH):
        lam_init = 0.8 - 0.6 * math.exp(-0.3 * l)
        lam = (jnp.exp(jnp.sum(lambda_q1[l].astype(f32) * lambda_k1[l].astype(f32)))
               - jnp.exp(jnp.sum(lambda_q2[l].astype(f32) * lambda_k2[l].astype(f32))) + lam_init)

        xp, k_l, v_l, r_l, c_l = layer(
            xp, c_prompt, pos_p, jnp.zeros((xp.shape[0], CONV_W - 1, 2 * D_FF), xp.dtype),
            lambda q, k, v: diff_attention_prompt(q, k, v, rel_bias, lam),
            lambda q, k, v: retention_prompt(q, k, v, log_gamma),
            w_ada[l], b_ada[l], g_mix[l], w_in[l], lam_init, g_sub_a[l], g_sub_r[l], w_out[l],
            g_ffn[l], w_up[l], w_conv[l], b_conv[l], w_down[l])
        kp.append(k_l); vp.append(v_l); rp.append(r_l); cp.append(c_l)

        xs, k_l, v_l, r_l, c_l = layer(
            xs, c_sample, pos_s, state_conv[l],
            lambda q, k, v: diff_attention_block(q, jnp.concatenate([cache_k[l], k], axis=1),
                                                 jnp.concatenate([cache_v[l], v], axis=1),
                                                 pos_s, k_pos_s, rel_bias, lam),
            lambda q, k, v: retention_chunk(state_ret[l].astype(f32), q, k, v, log_gamma),
            w_ada[l], b_ada[l], g_mix[l], w_in[l], lam_init, g_sub_a[l], g_sub_r[l], w_out[l],
            g_ffn[l], w_up[l], w_conv[l], b_conv[l], w_down[l])
        ksl.append(k_l); vsl.append(v_l); rsl.append(r_l); csl.append(c_l)

    y_prompt = rmsnorm(xp, g_final)
    y_sample = rmsnorm(xs, g_final)
    return (y_prompt, y_sample, jnp.stack(kp), jnp.stack(vp), jnp.stack(rp), jnp.stack(cp),
            jnp.stack(ksl), jnp.stack(vsl), jnp.stack(rsl), jnp.stack(csl))
```

```python
import functools
import math

import jax
import jax.numpy as jnp
from jax import lax
from jax.experimental import pallas as pl
from jax.experimental.pallas import tpu as pltpu

F32 = jnp.float32
BF16 = jnp.bfloat16

D_MODEL = 1024
CHUNK = 64
A_HEADS = 4
A_QK_DIM = 64
HEAD_W = 128
GROUP_W = A_HEADS * HEAD_W
N_GROUPS = 7
R_HEADS = 4
D_FF = 2816
CONV_W = 3
N_BUCKETS = 32
MAX_DISTANCE = 128
ROPE_BASE = 10000.0
EPS = 1e-6
NEG_INF = -1e30

VMEM_LIMIT = 56 * 1024 * 1024
FF_CHUNK = 256


def _cparams(n_axes):
    return pltpu.CompilerParams(dimension_semantics=("arbitrary",) * n_axes,
                                vmem_limit_bytes=VMEM_LIMIT)


def _resident(block_shape, index_map):
    return pl.BlockSpec(block_shape, index_map, pipeline_mode=pl.Buffered(1))


def _rms(x):
    return x * lax.rsqrt(jnp.mean(x * x, axis=-1, keepdims=True) + EPS)


def _silu(x):
    return x * jax.nn.sigmoid(x)


def _dot_nt(a, b):
    return lax.dot_general(a, b, (((1,), (1,)), ((), ())), preferred_element_type=F32)


def _dot_tn(a, b):
    return lax.dot_general(a, b, (((0,), (0,)), ((), ())), preferred_element_type=F32)


def _adaln_kernel(c_ref, w_ref, b_ref, o_ref):
    a = _silu(c_ref[...]).astype(BF16)
    o_ref[...] = jnp.dot(a, w_ref[...].astype(BF16), preferred_element_type=F32) + b_ref[...]


def _adaln(c, w_ada, b_ada):
    n_rows = c.shape[0]
    n_cols = w_ada.shape[1]
    tn = 1536
    return pl.pallas_call(
        _adaln_kernel,
        out_shape=jax.ShapeDtypeStruct((n_rows, n_cols), F32),
        grid=(n_cols // tn,),
        in_specs=[pl.BlockSpec((n_rows, D_MODEL), lambda j: (0, 0)),
                  pl.BlockSpec((D_MODEL, tn), lambda j: (0, j)),
                  pl.BlockSpec((1, tn), lambda j: (0, j))],
        out_specs=pl.BlockSpec((n_rows, tn), lambda j: (0, j)),
        compiler_params=_cparams(1),
        name="adaln",
    )(c, w_ada, b_ada.reshape(1, n_cols))


def _proj_kernel(x_ref, mod_ref, g_ref, w_ref, cos_ref, sin_ref,
                 k32_ref, v32_ref, qa_ref, ka_ref, va_ref, qr_ref, kr_ref, vr_ref, gr_ref):
    x = x_ref[0]
    h = _rms(x) * g_ref[...]
    h = h * (1.0 + mod_ref[0, 1]) + mod_ref[0, 0]
    hb = h.astype(BF16)

    def group(g):
        return jnp.dot(hb, w_ref[:, g * GROUP_W:(g + 1) * GROUP_W], preferred_element_type=F32)

    qa_ref[0] = (group(0) * (A_QK_DIM ** -0.5)).astype(BF16)
    ka = group(1)
    k32_ref[0] = ka
    ka_ref[0] = ka.astype(BF16)
    va = group(2)
    v32_ref[0] = va
    va_ref[0] = va.astype(BF16)

    cos = cos_ref[...]
    sin = sin_ref[...]

    def rotary(t, out_ref, scale):
        for hh in range(R_HEADS):
            th = t[:, hh * HEAD_W:(hh + 1) * HEAD_W]
            r = th * cos + pltpu.roll(th, HEAD_W // 2, 1) * sin
            if scale != 1.0:
                r = r * scale
            out_ref[0, :, hh * HEAD_W:(hh + 1) * HEAD_W] = r.astype(BF16)

    rotary(group(3), qr_ref, 1.0)
    rotary(group(4), kr_ref, HEAD_W ** -0.5)
    vr_ref[0] = group(5).astype(BF16)
    gr_ref[0] = group(6).astype(BF16)


def _proj(x, mod, g_mix, w_in_b, cos_t, sin_t, tm):
    b, length, _ = x.shape
    l_mod = mod.shape[2]
    mod_blk = 1 if l_mod == 1 else tm
    tok = lambda bi, i: (bi, i, 0)
    out_tok = pl.BlockSpec((1, tm, GROUP_W), tok)
    f32_out = jax.ShapeDtypeStruct((b, length, GROUP_W), F32)
    bf_out = jax.ShapeDtypeStruct((b, length, GROUP_W), BF16)
    return pl.pallas_call(
        _proj_kernel,
        out_shape=(f32_out, f32_out) + (bf_out,) * 7,
        grid=(b, length // tm),
        in_specs=[pl.BlockSpec((1, tm, D_MODEL), tok),
                  pl.BlockSpec((1, 6, mod_blk, D_MODEL),
                               (lambda bi, i: (bi, 0, 0, 0)) if l_mod == 1 else (lambda bi, i: (bi, 0, i, 0))),
                  pl.BlockSpec((1, D_MODEL), lambda bi, i: (0, 0)),
                  _resident((D_MODEL, N_GROUPS * GROUP_W), lambda bi, i: (0, 0)),
                  pl.BlockSpec((tm, HEAD_W), lambda bi, i: (i, 0)),
                  pl.BlockSpec((tm, HEAD_W), lambda bi, i: (i, 0))],
        out_specs=(out_tok,) * 9,
        compiler_params=_cparams(2),
        name="proj",
    )(x, mod, g_mix, w_in_b, cos_t, sin_t)


def _t5_bucket(rel):
    half = N_BUCKETS // 2
    max_exact = half // 2
    ret = jnp.where(rel > 0, half, 0)
    n = jnp.abs(rel)
    large = max_exact + (jnp.log(jnp.maximum(n, 1).astype(jnp.float32) / max_exact)
                         / math.log(MAX_DISTANCE / max_exact) * (half - max_exact)).astype(jnp.int32)
    large = jnp.minimum(large, half - 1)
    return ret + jnp.where(n < max_exact, n, large)


def _bias_table(q_pos, k_pos, rel_bias):
    bucket = _t5_bucket(k_pos[None, :] - q_pos[:, None])
    bias = jnp.transpose(rel_bias[bucket].astype(F32), (2, 0, 1))
    visible = (k_pos[None, :] // CHUNK) <= (q_pos[:, None] // CHUNK)
    return bias, visible


def _stack_query(q, rows):
    lane = lax.broadcasted_iota(jnp.int32, (rows, HEAD_W), 1)
    zero = jnp.zeros_like(q)
    return jnp.concatenate([jnp.where(lane < A_QK_DIM, q, zero),
                            jnp.where(lane >= A_QK_DIM, q, zero)], axis=0)


def _attn_out(o, g_ref, lam_init):
    return (_rms(o) * g_ref[...] * (1.0 - lam_init)).astype(BF16)


def _attn_prompt_kernel(lam_ref, q_ref, k_ref, v_ref, bias_ref, g_ref, o_ref,
                        qs_scr, m_scr, l_scr, acc_scr, *, t, lam_init):
    i = pl.program_id(2)
    qs_scr[...] = _stack_query(q_ref[0], t)
    m_scr[...] = jnp.full_like(m_scr, -jnp.inf)
    l_scr[...] = jnp.zeros_like(l_scr)
    acc_scr[...] = jnp.zeros_like(acc_scr)

    def update(k_start, bias):
        k = k_ref[0, pl.ds(k_start, t), :]
        v = v_ref[0, pl.ds(k_start, t), :]
        s = _dot_nt(qs_scr[...], k)
        if bias is not None:
            s = s + jnp.concatenate([bias, bias], axis=0)
        m_prev = m_scr[...]
        m_next = jnp.maximum(m_prev, jnp.max(s, axis=1, keepdims=True))
        alpha = jnp.exp(m_prev - m_next)
        p = jnp.exp(s - jnp.tile(m_next, (1, t // HEAD_W)))
        l_scr[...] = alpha * l_scr[...] + jnp.sum(p, axis=1, keepdims=True)
        m_scr[...] = m_next
        acc_scr[...] = alpha * acc_scr[...] + jnp.dot(p.astype(BF16), v, preferred_element_type=F32)

    def far_body(j, carry):
        update(pl.multiple_of(j * t, t), None)
        return carry

    lax.fori_loop(0, i - 1, far_body, 0)

    @pl.when(i >= 1)
    def _():
        update(pl.multiple_of((i - 1) * t, t), bias_ref[0, 1])

    update(pl.multiple_of(i * t, t), bias_ref[0, 0])

    o_all = acc_scr[...] / l_scr[...]
    o = o_all[:t] - lam_ref[0] * o_all[t:]
    o_ref[0] = _attn_out(o, g_ref, lam_init)


def _attn_prompt(lam, qa, ka, va, bias_tiles, g_sub_a, lam_init, t):
    b, s, _ = qa.shape
    kernel = functools.partial(_attn_prompt_kernel, t=t, lam_init=lam_init)
    return pl.pallas_call(
        kernel,
        out_shape=jax.ShapeDtypeStruct((b, s, GROUP_W), BF16),
        grid=(b, A_HEADS, s // t),
        in_specs=[pl.BlockSpec(memory_space=pltpu.SMEM),
                  pl.BlockSpec((1, t, HEAD_W), lambda bi, h, i: (bi, i, h)),
                  pl.BlockSpec((1, s, HEAD_W), lambda bi, h, i: (bi, 0, h)),
                  pl.BlockSpec((1, s, HEAD_W), lambda bi, h, i: (bi, 0, h)),
                  pl.BlockSpec((1, 2, t, t), lambda bi, h, i: (h, 0, 0, 0)),
                  pl.BlockSpec((1, HEAD_W), lambda bi, h, i: (0, 0))],
        out_specs=pl.BlockSpec((1, t, HEAD_W), lambda bi, h, i: (bi, i, h)),
        scratch_shapes=[pltpu.VMEM((2 * t, HEAD_W), BF16),
                        pltpu.VMEM((2 * t, HEAD_W), F32),
                        pltpu.VMEM((2 * t, HEAD_W), F32),
                        pltpu.VMEM((2 * t, HEAD_W), F32)],
        compiler_params=_cparams(3),
        name="attn_prompt",
    )(lam, qa, ka, va, bias_tiles, g_sub_a)


def _attn_sample_kernel(lam_ref, q_ref, kc_ref, vc_ref, kn_ref, vn_ref, bc_ref, bn_ref, g_ref, o_ref,
                        *, rows, lam_init):
    qs = _stack_query(q_ref[0], rows)
    kc = kc_ref[0].astype(BF16)
    vc = vc_ref[0].astype(BF16)
    bc = bc_ref[0]
    bn = bn_ref[0]
    s_c = _dot_nt(qs, kc) + jnp.concatenate([bc, bc], axis=0)
    s_n = _dot_nt(qs, kn_ref[0]) + jnp.concatenate([bn, bn], axis=0)
    m = jnp.maximum(jnp.max(s_c, axis=1, keepdims=True), jnp.max(s_n, axis=1, keepdims=True))
    p_c = jnp.exp(s_c - m)
    p_n = jnp.exp(s_n - m)
    inv_l = 1.0 / (jnp.sum(p_c, axis=1, keepdims=True) + jnp.sum(p_n, axis=1, keepdims=True))
    p_c = p_c * inv_l
    p_n = p_n * inv_l
    lam = lam_ref[0]
    d_c = (p_c[:rows] - lam * p_c[rows:]).astype(BF16)
    d_n = (p_n[:rows] - lam * p_n[rows:]).astype(BF16)
    o = (jnp.dot(d_c, vc, preferred_element_type=F32)
         + jnp.dot(d_n, vn_ref[0], preferred_element_type=F32))
    o_ref[0] = _attn_out(o, g_ref, lam_init)


def _attn_sample(lam, qa, cache_k, cache_v, ka, va, bias_c, bias_n, g_sub_a, lam_init):
    nb, rows, _ = qa.shape
    past = cache_k.shape[1]
    kernel = functools.partial(_attn_sample_kernel, rows=rows, lam_init=lam_init)
    new_spec = pl.BlockSpec((1, rows, HEAD_W), lambda bi, h: (bi, 0, h))
    cache_spec = pl.BlockSpec((1, past, HEAD_W), lambda bi, h: (bi, 0, h))
    return pl.pallas_call(
        kernel,
        out_shape=jax.ShapeDtypeStruct((nb, rows, GROUP_W), BF16),
        grid=(nb, A_HEADS),
        in_specs=[pl.BlockSpec(memory_space=pltpu.SMEM),
                  new_spec, cache_spec, cache_spec, new_spec, new_spec,
                  pl.BlockSpec((1, rows, past), lambda bi, h: (h, 0, 0)),
                  pl.BlockSpec((1, rows, rows), lambda bi, h: (h, 0, 0)),
                  pl.BlockSpec((1, HEAD_W), lambda bi, h: (0, 0))],
        out_specs=new_spec,
        compiler_params=_cparams(2),
        name="attn_sample",
    )(lam, qa, cache_k, cache_v, ka, va, bias_c, bias_n, g_sub_a)


def _retention_kernel(lg_ref, q_ref, k_ref, v_ref, g_ref, s0_ref, gsub_ref, o_ref, st_ref, *, c):
    @pl.when(pl.program_id(1) == 0)
    def _():
        st_ref[...] = s0_ref[...]

    row = lax.broadcasted_iota(jnp.int32, (c, c), 0)
    col = lax.broadcasted_iota(jnp.int32, (c, c), 1)
    diff = (row - col).astype(F32)
    causal = diff >= 0.0
    diff = jnp.maximum(diff, 0.0)
    idx = lax.broadcasted_iota(jnp.int32, (c, 1), 0).astype(F32)
    gsub = gsub_ref[...]

    for hh in range(R_HEADS):
        lg = lg_ref[hh]
        sl = slice(hh * HEAD_W, (hh + 1) * HEAD_W)
        q = q_ref[0, :, sl]
        k = k_ref[0, :, sl]
        v = v_ref[0, :, sl]
        decay = jnp.where(causal, jnp.exp(lg * diff), 0.0)
        scores = _dot_nt(q, k) * decay
        state = st_ref[0, hh]
        o = jnp.dot(scores.astype(BF16), v, preferred_element_type=F32)
        o = o + jnp.dot(q, state.astype(BF16), preferred_element_type=F32) * jnp.exp(lg * (idx + 1.0))
        kd = (k.astype(F32) * jnp.exp(lg * (c - 1.0 - idx))).astype(BF16)
        st_ref[0, hh] = jnp.exp(lg * c) * state + _dot_tn(kd, v)
        gate = g_ref[0, :, sl].astype(F32)
        o_ref[0, :, sl] = (_rms(o) * gsub * _silu(gate)).astype(BF16)


def _retention(log_gamma, qr, kr, vr, gr, state0, g_sub_r, c):
    b, s, _ = qr.shape
    kernel = functools.partial(_retention_kernel, c=c)
    tok = pl.BlockSpec((1, c, GROUP_W), lambda bi, i: (bi, i, 0))
    st = pl.BlockSpec((1, R_HEADS, HEAD_W, HEAD_W), lambda bi, i: (bi, 0, 0, 0))
    return pl.pallas_call(
        kernel,
        out_shape=(jax.ShapeDtypeStruct((b, s, GROUP_W), BF16),
                   jax.ShapeDtypeStruct((b, R_HEADS, HEAD_W, HEAD_W), F32)),
        grid=(b, s // c),
        in_specs=[pl.BlockSpec(memory_space=pltpu.SMEM), tok, tok, tok, tok, st,
                  pl.BlockSpec((1, HEAD_W), lambda bi, i: (0, 0))],
        out_specs=(tok, st),
        compiler_params=_cparams(2),
        name="retention",
    )(log_gamma, qr, kr, vr, gr, state0, g_sub_r)


def _mlp_kernel(x_ref, oa_ref, or_ref, mod_ref, wout_ref, gffn_ref, wup_ref, wconv_ref, bconv_ref,
                wdown_ref, gfin_ref, cprev_ref, y_ref, cst_ref, ue_scr, acc_scr, *, tm, shift):
    halo = (CONV_W - 1) * shift
    pad = -(-halo // 8) * 8

    @pl.when(pl.program_id(1) == 0)
    def _():
        cst_ref[...] = cprev_ref[...]

    mixed = (jnp.dot(oa_ref[0], wout_ref[:GROUP_W, :], preferred_element_type=F32)
             + jnp.dot(or_ref[0], wout_ref[GROUP_W:, :], preferred_element_type=F32))
    x1 = x_ref[0] + mod_ref[0, 2] * mixed
    h = _rms(x1) * gffn_ref[...]
    h = h * (1.0 + mod_ref[0, 4]) + mod_ref[0, 3]
    hb = h.astype(BF16)

    def conv_half(slot, col0):
        cols = slice(col0, col0 + FF_CHUNK)
        u = jnp.dot(hb, wup_ref[:, cols], preferred_element_type=F32)
        ue_scr[slot, pad - halo:pad, :] = cst_ref[0, :, cols]
        ue_scr[slot, pad:pad + tm, :] = u
        cst_ref[0, :, cols] = u[tm - halo:, :]
        y = wconv_ref[CONV_W - 1:CONV_W, cols] * u + bconv_ref[:, cols]
        for j in range(CONV_W - 1):
            start = pad - halo + j * shift
            y = y + wconv_ref[j:j + 1, cols] * ue_scr[slot, start:start + tm, :]
        return y

    for ch in range(D_FF // FF_CHUNK):
        ya = conv_half(0, ch * FF_CHUNK)
        yg = conv_half(1, D_FF + ch * FF_CHUNK)
        act = (_silu(ya) * yg).astype(BF16)
        part = jnp.dot(act, wdown_ref[ch * FF_CHUNK:(ch + 1) * FF_CHUNK, :], preferred_element_type=F32)
        if ch == 0:
            acc_scr[...] = part
        else:
            acc_scr[...] += part

    x2 = x1 + mod_ref[0, 5] * acc_scr[...]
    y_ref[0] = _rms(x2) * gfin_ref[...]


def _mlp(x, oa, orr, mod, w_out_b, g_ffn, w_up_b, w_conv, b_conv, w_down_b, g_final, conv_prev, tm, shift):
    b, length, _ = x.shape
    l_mod = mod.shape[2]
    mod_blk = 1 if l_mod == 1 else tm
    halo = (CONV_W - 1) * shift
    pad = -(-halo // 8) * 8
    kernel = functools.partial(_mlp_kernel, tm=tm, shift=shift)
    tok = lambda bi, i: (bi, i, 0)
    const = lambda bi, i: (0, 0)
    return pl.pallas_call(
        kernel,
        out_shape=(jax.ShapeDtypeStruct((b, length, D_MODEL), F32),
                   jax.ShapeDtypeStruct((b, halo, 2 * D_FF), F32)),
        grid=(b, length // tm),
        in_specs=[pl.BlockSpec((1, tm, D_MODEL), tok),
                  pl.BlockSpec((1, tm, GROUP_W), tok),
                  pl.BlockSpec((1, tm, GROUP_W), tok),
                  pl.BlockSpec((1, 6, mod_blk, D_MODEL),
                               (lambda bi, i: (bi, 0, 0, 0)) if l_mod == 1 else (lambda bi, i: (bi, 0, i, 0))),
                  _resident((2 * GROUP_W, D_MODEL), const),
                  pl.BlockSpec((1, D_MODEL), const),
                  _resident((D_MODEL, 2 * D_FF), const),
                  pl.BlockSpec((CONV_W, 2 * D_FF), const),
                  pl.BlockSpec((1, 2 * D_FF), const),
                  _resident((D_FF, D_MODEL), const),
                  pl.BlockSpec((1, D_MODEL), const),
                  pl.BlockSpec((1, halo, 2 * D_FF), lambda bi, i: (bi, 0, 0))],
        out_specs=(pl.BlockSpec((1, tm, D_MODEL), tok),
                   pl.BlockSpec((1, halo, 2 * D_FF), lambda bi, i: (bi, 0, 0))),
        scratch_shapes=[pltpu.VMEM((2, pad + tm, FF_CHUNK), F32),
                        pltpu.VMEM((tm, D_MODEL), F32)],
        compiler_params=_cparams(2),
        name="mlp",
    )(x, oa, orr, mod, w_out_b, g_ffn, w_up_b, w_conv, b_conv, w_down_b, g_final, conv_prev)


def _rotary_tables(pos):
    half = HEAD_W // 2
    inv_freq = ROPE_BASE ** (-jnp.arange(half, dtype=F32) / half)
    ang = pos.astype(F32)[:, None] * inv_freq[None, :]
    cos = jnp.cos(ang)
    sin = jnp.sin(ang)
    return jnp.concatenate([cos, cos], axis=-1), jnp.concatenate([-sin, sin], axis=-1)


def kernel(x_prompt, x_sample, cache_k, cache_v, state_ret, state_conv, c_prompt, c_sample,
           w_ada, b_ada, g_mix, w_in, lambda_q1, lambda_k1, lambda_q2, lambda_k2,
           g_sub_a, g_sub_r, w_out, g_ffn, w_up, w_conv, b_conv, w_down, rel_bias, g_final):
    depth = w_in.shape[0]
    assert depth == 1
    l = 0
    nb_p, s_p, _ = x_prompt.shape
    nb_s, s_new, _ = x_sample.shape
    past = cache_k.shape[2]
    n_tok_s = nb_s * s_new

    log_gamma = jnp.log(1.0 - 2.0 ** (-5.0 - jnp.arange(R_HEADS, dtype=F32)))
    lam_init = 0.8 - 0.6 * math.exp(-0.3 * l)
    lam = (jnp.exp(jnp.sum(lambda_q1[l].astype(F32) * lambda_k1[l].astype(F32)))
           - jnp.exp(jnp.sum(lambda_q2[l].astype(F32) * lambda_k2[l].astype(F32))) + lam_init).reshape(1)

    w_in_b = w_in[l].astype(BF16)
    w_out_b = w_out[l].astype(BF16)
    w_up_b = w_up[l].astype(BF16)
    w_down_b = w_down[l].astype(BF16)
    g_mix_l = g_mix[l].reshape(1, D_MODEL)
    g_ffn_l = g_ffn[l].reshape(1, D_MODEL)
    g_fin = g_final.reshape(1, D_MODEL)
    g_sa = g_sub_a[l].reshape(1, HEAD_W)
    g_sr = g_sub_r[l].reshape(1, HEAD_W)
    w_conv_l = w_conv[l]
    b_conv_l = b_conv[l].reshape(1, 2 * D_FF)

    mods = _adaln(jnp.concatenate([c_prompt, c_sample], axis=0), w_ada[l], b_ada[l])
    mods = mods.reshape(nb_p + nb_s, 6, D_MODEL)
    mod_p = mods[:nb_p].reshape(nb_p, 6, 1, D_MODEL)
    mod_s = jnp.transpose(mods[nb_p:], (1, 0, 2))
    mod_s_stream = jnp.repeat(mod_s, s_new, axis=1)[None]
    mod_s_time = jnp.tile(mod_s, (1, s_new, 1))[None]

    t_attn = 256
    cos_p, sin_p = _rotary_tables(jnp.arange(s_p, dtype=jnp.int32))
    (k32_p, v32_p, qa_p, ka_p, va_p, qr_p, kr_p, vr_p, gr_p) = _proj(
        x_prompt, mod_p, g_mix_l, w_in_b, cos_p, sin_p, tm=512)

    q_pos = jnp.arange(t_attn, dtype=jnp.int32) + t_attn
    far = rel_bias[N_BUCKETS // 2 - 1].astype(F32)[:, None, None]
    b_diag, vis_diag = _bias_table(q_pos, q_pos, rel_bias)
    b_sub, _ = _bias_table(q_pos, q_pos - t_attn, rel_bias)
    bias_tiles = jnp.stack([jnp.where(vis_diag, b_diag - far, NEG_INF), b_sub - far], axis=1)
    oa_p = _attn_prompt(lam, qa_p, ka_p, va_p, bias_tiles, g_sa, lam_init, t_attn)

    or_p, ret_p = _retention(log_gamma, qr_p, kr_p, vr_p, gr_p,
                             jnp.zeros((nb_p, R_HEADS, HEAD_W, HEAD_W), F32), g_sr, c=256)

    y_p, conv_p = _mlp(x_prompt, oa_p, or_p, mod_p, w_out_b, g_ffn_l, w_up_b, w_conv_l, b_conv_l,
                       w_down_b, g_fin, jnp.zeros((nb_p, CONV_W - 1, 2 * D_FF), F32), tm=512, shift=1)

    pos_s = past + jnp.arange(s_new, dtype=jnp.int32)
    cos_s, sin_s = _rotary_tables(pos_s)
    cos_s = jnp.tile(cos_s, (nb_s, 1))
    sin_s = jnp.tile(sin_s, (nb_s, 1))
    (k32_s, v32_s, qa_s, ka_s, va_s, qr_s, kr_s, vr_s, gr_s) = _proj(
        x_sample.reshape(1, n_tok_s, D_MODEL), mod_s_stream, g_mix_l, w_in_b, cos_s, sin_s, tm=n_tok_s)

    def streams(t):
        return t.reshape(nb_s, s_new, GROUP_W)

    k_pos_s = jnp.arange(past + s_new, dtype=jnp.int32)
    b_s, vis_s = _bias_table(pos_s, k_pos_s, rel_bias)
    b_s = jnp.where(vis_s, b_s, NEG_INF)
    oa_s = _attn_sample(lam, streams(qa_s), cache_k[l].reshape(nb_s, past, GROUP_W),
                        cache_v[l].reshape(nb_s, past, GROUP_W), streams(ka_s), streams(va_s),
                        b_s[:, :, :past], b_s[:, :, past:], g_sa, lam_init)

    or_s, ret_s = _retention(log_gamma, streams(qr_s), streams(kr_s), streams(vr_s), streams(gr_s),
                             state_ret[l].astype(F32), g_sr, c=s_new)

    def time_major(t):
        return jnp.transpose(t, (1, 0, 2)).reshape(1, n_tok_s, t.shape[-1])

    conv_prev_s = jnp.transpose(state_conv[l], (1, 0, 2)).reshape(1, (CONV_W - 1) * nb_s, 2 * D_FF)
    y_s, conv_s = _mlp(time_major(x_sample), time_major(oa_s), time_major(or_s), mod_s_time,
                       w_out_b, g_ffn_l, w_up_b, w_conv_l, b_conv_l, w_down_b, g_fin, conv_prev_s,
                       tm=n_tok_s, shift=nb_s)
    y_s = jnp.transpose(y_s.reshape(s_new, nb_s, D_MODEL), (1, 0, 2))
    conv_s = jnp.transpose(conv_s.reshape(CONV_W - 1, nb_s, 2 * D_FF), (1, 0, 2))

    def heads(t, n):
        return t.reshape(1, n, -1, A_HEADS, HEAD_W)

    return (y_p, y_s, heads(k32_p, nb_p), heads(v32_p, nb_p), ret_p[None], conv_p[None],
            heads(k32_s, nb_s), heads(v32_s, nb_s), ret_s[None], conv_s[None])
```

```python
import functools
import math

import jax
import jax.numpy as jnp
from jax import lax
from jax.experimental import pallas as pl
from jax.experimental.pallas import tpu as pltpu

F32 = jnp.float32
BF16 = jnp.bfloat16

D_MODEL = 1024
CHUNK = 64
A_HEADS = 4
A_QK_DIM = 64
HEAD_W = 128
GROUP_W = A_HEADS * HEAD_W
N_GROUPS = 7
R_HEADS = 4
D_FF = 2816
CONV_W = 3
N_BUCKETS = 32
FAR_BUCKET = N_BUCKETS // 2 - 1
MAX_DISTANCE = 128
ROPE_BASE = 10000.0
EPS = 1e-6
NEG_INF = -1e30

VMEM_LIMIT = 56 * 1024 * 1024
FF_CHUNK = 256
LOG2E = math.log2(math.e)
Q_SCALE = A_QK_DIM ** -0.5 * LOG2E


def _cparams(n_axes):
    return pltpu.CompilerParams(dimension_semantics=("arbitrary",) * n_axes,
                                vmem_limit_bytes=VMEM_LIMIT)


def _resident(block_shape, index_map):
    return pl.BlockSpec(block_shape, index_map, pipeline_mode=pl.Buffered(1))


def _rms(x):
    return x * lax.rsqrt(jnp.mean(x * x, axis=-1, keepdims=True) + EPS)


def _silu(x):
    return x * jax.nn.sigmoid(x)


def _dot_nt(a, b):
    return lax.dot_general(a, b, (((1,), (1,)), ((), ())), preferred_element_type=F32)


def _dot_tn(a, b):
    return lax.dot_general(a, b, (((0,), (0,)), ((), ())), preferred_element_type=F32)


def _adaln_kernel(c_ref, w_ref, b_ref, o_ref):
    a = _silu(c_ref[...]).astype(BF16)
    o_ref[...] = jnp.dot(a, w_ref[...].astype(BF16), preferred_element_type=F32) + b_ref[...]


def _adaln(c, w_ada, b_ada):
    n_rows = c.shape[0]
    n_cols = w_ada.shape[1]
    tn = 1536
    return pl.pallas_call(
        _adaln_kernel,
        out_shape=jax.ShapeDtypeStruct((n_rows, n_cols), F32),
        grid=(n_cols // tn,),
        in_specs=[pl.BlockSpec((n_rows, D_MODEL), lambda j: (0, 0)),
                  pl.BlockSpec((D_MODEL, tn), lambda j: (0, j)),
                  pl.BlockSpec((1, tn), lambda j: (0, j))],
        out_specs=pl.BlockSpec((n_rows, tn), lambda j: (0, j)),
        compiler_params=_cparams(1),
        name="adaln",
    )(c, w_ada, b_ada.reshape(1, n_cols))


def _proj_kernel(x_ref, mod_ref, g_ref, w_ref, cos_ref, sin_ref,
                 k32_ref, v32_ref, qa_ref, ka_ref, va_ref, qr_ref, kr_ref, vr_ref, gr_ref, *maybe_vt_ref,
                 tm, t_attn):
    x = x_ref[0]
    h = _rms(x) * g_ref[...]
    h = h * (1.0 + mod_ref[0, 1]) + mod_ref[0, 0]
    hb = h.astype(BF16)

    def group(g):
        return jnp.dot(hb, w_ref[:, g * GROUP_W:(g + 1) * GROUP_W], preferred_element_type=F32)

    def store_heads(ref, t):
        for hh in range(A_HEADS):
            ref[0, pl.ds(hh, tm, stride=A_HEADS), :] = t[:, hh * HEAD_W:(hh + 1) * HEAD_W]

    qa_ref[0] = (group(0) * Q_SCALE).astype(BF16)
    ka = group(1)
    store_heads(k32_ref, ka)
    ka_ref[0] = ka.astype(BF16)
    va = group(2)
    store_heads(v32_ref, va)
    va_ref[0] = va.astype(BF16)
    if maybe_vt_ref:
        for a in range(tm // t_attn):
            maybe_vt_ref[0][0, a] = va[a * t_attn:(a + 1) * t_attn, :].T.astype(BF16)

    cos = cos_ref[...]
    sin = sin_ref[...]

    def rotary(t, out_ref, scale):
        for hh in range(R_HEADS):
            th = t[:, hh * HEAD_W:(hh + 1) * HEAD_W]
            r = th * cos + pltpu.roll(th, HEAD_W // 2, 1) * sin
            if scale != 1.0:
                r = r * scale
            out_ref[0, :, hh * HEAD_W:(hh + 1) * HEAD_W] = r.astype(BF16)

    rotary(group(3), qr_ref, 1.0)
    rotary(group(4), kr_ref, HEAD_W ** -0.5)
    vr_ref[0] = group(5).astype(BF16)
    gr_ref[0] = group(6).astype(BF16)


def _proj(x, mod, g_mix, w_in_b, cos_t, sin_t, tm, t_attn=None):
    b, length, _ = x.shape
    l_mod = mod.shape[2]
    mod_blk = 1 if l_mod == 1 else tm
    tok = lambda bi, i: (bi, i, 0)
    out_tok = pl.BlockSpec((1, tm, GROUP_W), tok)
    out_heads = pl.BlockSpec((1, tm * A_HEADS, HEAD_W), tok)
    heads_out = jax.ShapeDtypeStruct((b, length * A_HEADS, HEAD_W), F32)
    bf_out = jax.ShapeDtypeStruct((b, length, GROUP_W), BF16)
    out_shape = (heads_out, heads_out) + (bf_out,) * 7
    out_specs = (out_heads, out_heads) + (out_tok,) * 7
    if t_attn is not None:
        out_shape += (jax.ShapeDtypeStruct((b, length // t_attn, GROUP_W, t_attn), BF16),)
        out_specs += (pl.BlockSpec((1, tm // t_attn, GROUP_W, t_attn), lambda bi, i: (bi, i, 0, 0)),)
    return pl.pallas_call(
        functools.partial(_proj_kernel, tm=tm, t_attn=t_attn),
        out_shape=out_shape,
        grid=(b, length // tm),
        in_specs=[pl.BlockSpec((1, tm, D_MODEL), tok),
                  pl.BlockSpec((1, 6, mod_blk, D_MODEL),
                               (lambda bi, i: (bi, 0, 0, 0)) if l_mod == 1 else (lambda bi, i: (bi, 0, i, 0))),
                  pl.BlockSpec((1, D_MODEL), lambda bi, i: (0, 0)),
                  _resident((D_MODEL, N_GROUPS * GROUP_W), lambda bi, i: (0, 0)),
                  pl.BlockSpec((tm, HEAD_W), lambda bi, i: (i, 0)),
                  pl.BlockSpec((tm, HEAD_W), lambda bi, i: (i, 0))],
        out_specs=out_specs,
        compiler_params=_cparams(2),
        name="proj",
    )(x, mod, g_mix, w_in_b, cos_t, sin_t)


def _t5_bucket(rel):
    half = N_BUCKETS // 2
    max_exact = half // 2
    ret = jnp.where(rel > 0, half, 0)
    n = jnp.abs(rel)
    large = max_exact + (jnp.log(jnp.maximum(n, 1).astype(jnp.float32) / max_exact)
                         / math.log(MAX_DISTANCE / max_exact) * (half - max_exact)).astype(jnp.int32)
    large = jnp.minimum(large, half - 1)
    return ret + jnp.where(n < max_exact, n, large)


def _bias_kernel(rb_ref, bucket_ref, o_ref, *, row_blk, shift):
    def rows(r, carry):
        sl = pl.ds(pl.multiple_of(r * row_blk, row_blk), row_blk)
        b = bucket_ref[0, sl, :]
        acc = [jnp.zeros(b.shape, F32) for _ in range(A_HEADS)]
        for n in range(N_BUCKETS):
            hit = b == n
            for hh in range(A_HEADS):
                acc[hh] = jnp.where(hit, rb_ref[n * A_HEADS + hh], acc[hh])
        for hh in range(A_HEADS):
            val = acc[hh] - rb_ref[FAR_BUCKET * A_HEADS + hh] if shift else acc[hh]
            o_ref[hh, 0, sl, :] = jnp.where(b < 0, NEG_INF, val * LOG2E)
        return carry

    lax.fori_loop(0, bucket_ref.shape[1] // row_blk, rows, 0)


def _bias(rel_bias, bucket, shift):
    n, r, c = bucket.shape
    row_blk = 16 if c <= 1024 else 8
    return pl.pallas_call(
        functools.partial(_bias_kernel, row_blk=row_blk, shift=shift),
        out_shape=jax.ShapeDtypeStruct((A_HEADS, n, r, c), F32),
        grid=(n,),
        in_specs=[pl.BlockSpec(memory_space=pltpu.SMEM),
                  pl.BlockSpec((1, r, c), lambda j: (j, 0, 0))],
        out_specs=pl.BlockSpec((A_HEADS, 1, r, c), lambda j: (0, j, 0, 0)),
        compiler_params=_cparams(1),
        name="bias",
    )(rel_bias.astype(F32).reshape(N_BUCKETS * A_HEADS), bucket)


def _stack_query(q, rows):
    lane = lax.broadcasted_iota(jnp.int32, (rows, HEAD_W), 1)
    zero = jnp.zeros_like(q)
    return jnp.concatenate([jnp.where(lane < A_QK_DIM, q, zero),
                            jnp.where(lane >= A_QK_DIM, q, zero)], axis=0)


def _attn_prompt_kernel(lam_ref, q_ref, k_ref, vt_ref, bias_ref, g_ref, o_ref, qs_scr, acc_scr,
                        *, t, lam_init):
    i = pl.program_id(2)
    qs_scr[...] = _stack_query(q_ref[0], t)
    acc_scr[...] = jnp.zeros_like(acc_scr)

    def scores(j):
        k = k_ref[0, pl.ds(pl.multiple_of(j * t, t), t), :]
        return _dot_nt(k, qs_scr[...]) + bias_ref[0, jnp.minimum(i - j, 2)]

    def fold(j, s, m_prev, l_prev):
        m_next = jnp.maximum(m_prev, jnp.max(s, axis=0, keepdims=True))
        alpha = jnp.exp2(m_prev - m_next)
        p = jnp.exp2(s - m_next)
        l_next = alpha * l_prev + jnp.sum(p, axis=0, keepdims=True)
        acc_scr[...] = alpha * acc_scr[...] + jnp.dot(vt_ref[0, j], p.astype(BF16),
                                                      preferred_element_type=F32)
        return m_next, l_next

    def body(j, carry):
        s, m, l = carry
        s_next = scores(j + 1)
        m, l = fold(j, s, m, l)
        return s_next, m, l

    m0 = jnp.full((1, 2 * t), -jnp.inf, F32)
    l0 = jnp.zeros((1, 2 * t), F32)
    s, m, l = lax.fori_loop(0, i, body, (scores(0), m0, l0))
    m, l = fold(i, s, m, l)

    o_all = acc_scr[...] / l
    o = o_all[:, :t] - lam_ref[0] * o_all[:, t:]
    o = o * lax.rsqrt(jnp.mean(o * o, axis=0, keepdims=True) + EPS) * (g_ref[...] * (1.0 - lam_init))
    o_ref[0] = o.T.astype(BF16)


def _attn_prompt(lam, qa, ka, vt, bias_tiles, g_col, lam_init, t):
    b, s, _ = qa.shape
    kernel = functools.partial(_attn_prompt_kernel, t=t, lam_init=lam_init)
    return pl.pallas_call(
        kernel,
        out_shape=jax.ShapeDtypeStruct((b, s, GROUP_W), BF16),
        grid=(b, A_HEADS, s // t),
        in_specs=[pl.BlockSpec(memory_space=pltpu.SMEM),
                  pl.BlockSpec((1, t, HEAD_W), lambda bi, h, i: (bi, i, h)),
                  pl.BlockSpec((1, s, HEAD_W), lambda bi, h, i: (bi, 0, h)),
                  pl.BlockSpec((1, s // t, HEAD_W, t), lambda bi, h, i: (bi, 0, h, 0)),
                  pl.BlockSpec((1, 3, t, 2 * t), lambda bi, h, i: (h, 0, 0, 0)),
                  pl.BlockSpec((HEAD_W, 1), lambda bi, h, i: (0, 0))],
        out_specs=pl.BlockSpec((1, t, HEAD_W), lambda bi, h, i: (bi, i, h)),
        scratch_shapes=[pltpu.VMEM((2 * t, HEAD_W), BF16),
                        pltpu.VMEM((HEAD_W, 2 * t), F32)],
        compiler_params=_cparams(3),
        name="attn_prompt",
    )(lam, qa, ka, vt, bias_tiles, g_col)


def _attn_sample_kernel(lam_ref, q_ref, kc_ref, vc_ref, kn_ref, vn_ref, bc_ref, bn_ref, g_ref, o_ref,
                        *, rows, past, lam_init):
    lam = lam_ref[0]
    for hh in range(A_HEADS):
        sl = slice(hh * HEAD_W, (hh + 1) * HEAD_W)
        qs = _stack_query(q_ref[0, :, sl], rows)
        kc = kc_ref[0, pl.ds(hh, past, stride=A_HEADS), :].astype(BF16)
        vc = vc_ref[0, pl.ds(hh, past, stride=A_HEADS), :].astype(BF16)
        bc = bc_ref[hh, 0]
        bn = bn_ref[hh, 0]
        s_c = _dot_nt(qs, kc) + jnp.concatenate([bc, bc], axis=0)
        s_n = _dot_nt(qs, kn_ref[0, :, sl]) + jnp.concatenate([bn, bn], axis=0)
        m = jnp.maximum(jnp.max(s_c, axis=1, keepdims=True), jnp.max(s_n, axis=1, keepdims=True))
        p_c = jnp.exp2(s_c - m)
        p_n = jnp.exp2(s_n - m)
        inv_l = 1.0 / (jnp.sum(p_c, axis=1, keepdims=True) + jnp.sum(p_n, axis=1, keepdims=True))
        p_c = p_c * inv_l
        p_n = p_n * inv_l
        d_c = (p_c[:rows] - lam * p_c[rows:]).astype(BF16)
        d_n = (p_n[:rows] - lam * p_n[rows:]).astype(BF16)
        o = (jnp.dot(d_c, vc, preferred_element_type=F32)
             + jnp.dot(d_n, vn_ref[0, :, sl], preferred_element_type=F32))
        o_ref[0, :, sl] = (_rms(o) * g_ref[...] * (1.0 - lam_init)).astype(BF16)


def _attn_sample(lam, qa, cache_k, cache_v, ka, va, bias_c, bias_n, g_row, lam_init):
    nb, rows, _ = qa.shape
    past = cache_k.shape[1] // A_HEADS
    kernel = functools.partial(_attn_sample_kernel, rows=rows, past=past, lam_init=lam_init)
    new_spec = pl.BlockSpec((1, rows, GROUP_W), lambda bi: (bi, 0, 0))
    cache_spec = pl.BlockSpec((1, past * A_HEADS, HEAD_W), lambda bi: (bi, 0, 0))
    return pl.pallas_call(
        kernel,
        out_shape=jax.ShapeDtypeStruct((nb, rows, GROUP_W), BF16),
        grid=(nb,),
        in_specs=[pl.BlockSpec(memory_space=pltpu.SMEM),
                  new_spec, cache_spec, cache_spec, new_spec, new_spec,
                  pl.BlockSpec((A_HEADS, 1, rows, past), lambda bi: (0, 0, 0, 0)),
                  pl.BlockSpec((A_HEADS, 1, rows, rows), lambda bi: (0, 0, 0, 0)),
                  pl.BlockSpec((1, HEAD_W), lambda bi: (0, 0))],
        out_specs=new_spec,
        compiler_params=_cparams(1),
        name="attn_sample",
    )(lam, qa, cache_k, cache_v, ka, va, bias_c, bias_n, g_row)


def _retention_kernel(lg_ref, q_ref, k_ref, v_ref, g_ref, s0_ref, gsub_ref, o_ref, st_ref, *, c):
    @pl.when(pl.program_id(1) == 0)
    def _():
        st_ref[...] = s0_ref[...]

    row = lax.broadcasted_iota(jnp.int32, (c, c), 0)
    col = lax.broadcasted_iota(jnp.int32, (c, c), 1)
    diff = (row - col).astype(F32)
    causal = diff >= 0.0
    diff = jnp.maximum(diff, 0.0)
    idx = lax.broadcasted_iota(jnp.int32, (c, 1), 0).astype(F32)
    gsub = gsub_ref[...]

    for hh in range(R_HEADS):
        lg = lg_ref[hh]
        sl = slice(hh * HEAD_W, (hh + 1) * HEAD_W)
        q = q_ref[0, :, sl]
        k = k_ref[0, :, sl]
        v = v_ref[0, :, sl]
        decay = jnp.where(causal, jnp.exp(lg * diff), 0.0)
        scores = _dot_nt(q, k) * decay
        state = st_ref[0, hh]
        o = jnp.dot(scores.astype(BF16), v, preferred_element_type=F32)
        o = o + jnp.dot(q, state.astype(BF16), preferred_element_type=F32) * jnp.exp(lg * (idx + 1.0))
        kd = (k.astype(F32) * jnp.exp(lg * (c - 1.0 - idx))).astype(BF16)
        st_ref[0, hh] = jnp.exp(lg * c) * state + _dot_tn(kd, v)
        gate = g_ref[0, :, sl].astype(F32)
        o_ref[0, :, sl] = (_rms(o) * gsub * _silu(gate)).astype(BF16)


def _retention(log_gamma, qr, kr, vr, gr, state0, g_sub_r, c):
    b, s, _ = qr.shape
    kernel = functools.partial(_retention_kernel, c=c)
    tok = pl.BlockSpec((1, c, GROUP_W), lambda bi, i: (bi, i, 0))
    st = pl.BlockSpec((1, R_HEADS, HEAD_W, HEAD_W), lambda bi, i: (bi, 0, 0, 0))
    return pl.pallas_call(
        kernel,
        out_shape=(jax.ShapeDtypeStruct((b, s, GROUP_W), BF16),
                   jax.ShapeDtypeStruct((b, R_HEADS, HEAD_W, HEAD_W), F32)),
        grid=(b, s // c),
        in_specs=[pl.BlockSpec(memory_space=pltpu.SMEM), tok, tok, tok, tok, st,
                  pl.BlockSpec((1, HEAD_W), lambda bi, i: (0, 0))],
        out_specs=(tok, st),
        compiler_params=_cparams(2),
        name="retention",
    )(log_gamma, qr, kr, vr, gr, state0, g_sub_r)


def _mlp_kernel(x_ref, oa_ref, or_ref, mod_ref, wout_ref, gffn_ref, wup_ref, wconv_ref, bconv_ref,
                wdown_ref, gfin_ref, cprev_ref, y_ref, cst_ref, ue_scr, acc_scr, *, tm, shift):
    halo = (CONV_W - 1) * shift
    pad = -(-halo // 8) * 8

    @pl.when(pl.program_id(1) == 0)
    def _():
        cst_ref[...] = cprev_ref[...]

    mixed = (jnp.dot(oa_ref[0], wout_ref[:GROUP_W, :], preferred_element_type=F32)
             + jnp.dot(or_ref[0], wout_ref[GROUP_W:, :], preferred_element_type=F32))
    x1 = x_ref[0] + mod_ref[0, 2] * mixed
    h = _rms(x1) * gffn_ref[...]
    h = h * (1.0 + mod_ref[0, 4]) + mod_ref[0, 3]
    hb = h.astype(BF16)

    def conv_half(slot, col0):
        cols = slice(col0, col0 + FF_CHUNK)
        u = jnp.dot(hb, wup_ref[:, cols], preferred_element_type=F32)
        ue_scr[slot, pad - halo:pad, :] = cst_ref[0, :, cols]
        ue_scr[slot, pad:pad + tm, :] = u
        cst_ref[0, :, cols] = u[tm - halo:, :]
        y = wconv_ref[CONV_W - 1:CONV_W, cols] * u + bconv_ref[:, cols]
        for j in range(CONV_W - 1):
            start = pad - halo + j * shift
            y = y + wconv_ref[j:j + 1, cols] * ue_scr[slot, start:start + tm, :]
        return y

    for ch in range(D_FF // FF_CHUNK):
        ya = conv_half(0, ch * FF_CHUNK)
        yg = conv_half(1, D_FF + ch * FF_CHUNK)
        act = (_silu(ya) * yg).astype(BF16)
        part = jnp.dot(act, wdown_ref[ch * FF_CHUNK:(ch + 1) * FF_CHUNK, :], preferred_element_type=F32)
        if ch == 0:
            acc_scr[...] = part
        else:
            acc_scr[...] += part

    x2 = x1 + mod_ref[0, 5] * acc_scr[...]
    y_ref[0] = _rms(x2) * gfin_ref[...]


def _mlp(x, oa, orr, mod, w_out_b, g_ffn, w_up_b, w_conv, b_conv, w_down_b, g_final, conv_prev, tm, shift):
    b, length, _ = x.shape
    l_mod = mod.shape[2]
    mod_blk = 1 if l_mod == 1 else tm
    halo = (CONV_W - 1) * shift
    pad = -(-halo // 8) * 8
    kernel = functools.partial(_mlp_kernel, tm=tm, shift=shift)
    tok = lambda bi, i: (bi, i, 0)
    const = lambda bi, i: (0, 0)
    return pl.pallas_call(
        kernel,
        out_shape=(jax.ShapeDtypeStruct((b, length, D_MODEL), F32),
                   jax.ShapeDtypeStruct((b, halo, 2 * D_FF), F32)),
        grid=(b, length // tm),
        in_specs=[pl.BlockSpec((1, tm, D_MODEL), tok),
                  pl.BlockSpec((1, tm, GROUP_W), tok),
                  pl.BlockSpec((1, tm, GROUP_W), tok),
                  pl.BlockSpec((1, 6, mod_blk, D_MODEL),
                               (lambda bi, i: (bi, 0, 0, 0)) if l_mod == 1 else (lambda bi, i: (bi, 0, i, 0))),
                  _resident((2 * GROUP_W, D_MODEL), const),
                  pl.BlockSpec((1, D_MODEL), const),
                  _resident((D_MODEL, 2 * D_FF), const),
                  pl.BlockSpec((CONV_W, 2 * D_FF), const),
                  pl.BlockSpec((1, 2 * D_FF), const),
                  _resident((D_FF, D_MODEL), const),
                  pl.BlockSpec((1, D_MODEL), const),
                  pl.BlockSpec((1, halo, 2 * D_FF), lambda bi, i: (bi, 0, 0))],
        out_specs=(pl.BlockSpec((1, tm, D_MODEL), tok),
                   pl.BlockSpec((1, halo, 2 * D_FF), lambda bi, i: (bi, 0, 0))),
        scratch_shapes=[pltpu.VMEM((2, pad + tm, FF_CHUNK), F32),
                        pltpu.VMEM((tm, D_MODEL), F32)],
        compiler_params=_cparams(2),
        name="mlp",
    )(x, oa, orr, mod, w_out_b, g_ffn, w_up_b, w_conv, b_conv, w_down_b, g_final, conv_prev)


def _rotary_tables(pos):
    half = HEAD_W // 2
    inv_freq = ROPE_BASE ** (-jnp.arange(half, dtype=F32) / half)
    ang = pos.astype(F32)[:, None] * inv_freq[None, :]
    cos = jnp.cos(ang)
    sin = jnp.sin(ang)
    return jnp.concatenate([cos, cos], axis=-1), jnp.concatenate([-sin, sin], axis=-1)


def _masked_bucket(q_pos, k_pos, keys_major):
    rel = k_pos[None, :] - q_pos[:, None]
    visible = (k_pos[None, :] // CHUNK) <= (q_pos[:, None] // CHUNK)
    bucket = jnp.where(visible, _t5_bucket(rel), -1).astype(jnp.int32)
    return bucket.T if keys_major else bucket


def kernel(x_prompt, x_sample, cache_k, cache_v, state_ret, state_conv, c_prompt, c_sample,
           w_ada, b_ada, g_mix, w_in, lambda_q1, lambda_k1, lambda_q2, lambda_k2,
           g_sub_a, g_sub_r, w_out, g_ffn, w_up, w_conv, b_conv, w_down, rel_bias, g_final):
    depth = w_in.shape[0]
    assert depth == 1
    l = 0
    nb_p, s_p, _ = x_prompt.shape
    nb_s, s_new, _ = x_sample.shape
    past = cache_k.shape[2]
    n_tok_s = nb_s * s_new

    log_gamma = jnp.log(1.0 - 2.0 ** (-5.0 - jnp.arange(R_HEADS, dtype=F32)))
    lam_init = 0.8 - 0.6 * math.exp(-0.3 * l)
    lam = (jnp.exp(jnp.sum(lambda_q1[l].astype(F32) * lambda_k1[l].astype(F32)))
           - jnp.exp(jnp.sum(lambda_q2[l].astype(F32) * lambda_k2[l].astype(F32))) + lam_init).reshape(1)

    w_in_b = w_in[l].astype(BF16)
    w_out_b = w_out[l].astype(BF16)
    w_up_b = w_up[l].astype(BF16)
    w_down_b = w_down[l].astype(BF16)
    g_mix_l = g_mix[l].reshape(1, D_MODEL)
    g_ffn_l = g_ffn[l].reshape(1, D_MODEL)
    g_fin = g_final.reshape(1, D_MODEL)
    g_sa_row = g_sub_a[l].reshape(1, HEAD_W)
    g_sa_col = g_sub_a[l].reshape(HEAD_W, 1)
    g_sr = g_sub_r[l].reshape(1, HEAD_W)
    w_conv_l = w_conv[l]
    b_conv_l = b_conv[l].reshape(1, 2 * D_FF)

    mods = _adaln(jnp.concatenate([c_prompt, c_sample], axis=0), w_ada[l], b_ada[l])
    mods = mods.reshape(nb_p + nb_s, 6, D_MODEL)
    mod_p = mods[:nb_p].reshape(nb_p, 6, 1, D_MODEL)
    mod_s = jnp.transpose(mods[nb_p:], (1, 0, 2))
    mod_s_stream = jnp.repeat(mod_s, s_new, axis=1)[None]
    mod_s_time = jnp.tile(mod_s, (1, s_new, 1))[None]

    t_attn = 256
    cos_p, sin_p = _rotary_tables(jnp.arange(s_p, dtype=jnp.int32))
    (k32_p, v32_p, qa_p, ka_p, _, qr_p, kr_p, vr_p, gr_p, vt_p) = _proj(
        x_prompt, mod_p, g_mix_l, w_in_b, cos_p, sin_p, tm=512, t_attn=t_attn)

    tile_pos = jnp.arange(t_attn, dtype=jnp.int32) + t_attn
    bucket_tiles = jnp.stack([_masked_bucket(tile_pos, tile_pos, True),
                              _masked_bucket(tile_pos, tile_pos - t_attn, True),
                              jnp.full((t_attn, t_attn), FAR_BUCKET, jnp.int32)])
    bias_tiles = _bias(rel_bias, jnp.tile(bucket_tiles, (1, 1, 2)), shift=True)
    oa_p = _attn_prompt(lam, qa_p, ka_p, vt_p, bias_tiles, g_sa_col, lam_init, t_attn)

    or_p, ret_p = _retention(log_gamma, qr_p, kr_p, vr_p, gr_p,
                             jnp.zeros((nb_p, R_HEADS, HEAD_W, HEAD_W), F32), g_sr, c=256)

    y_p, conv_p = _mlp(x_prompt, oa_p, or_p, mod_p, w_out_b, g_ffn_l, w_up_b, w_conv_l, b_conv_l,
                       w_down_b, g_fin, jnp.zeros((nb_p, CONV_W - 1, 2 * D_FF), F32), tm=512, shift=1)

    pos_s = past + jnp.arange(s_new, dtype=jnp.int32)
    cos_s, sin_s = _rotary_tables(pos_s)
    cos_s = jnp.tile(cos_s, (nb_s, 1))
    sin_s = jnp.tile(sin_s, (nb_s, 1))
    (k32_s, v32_s, qa_s, ka_s, va_s, qr_s, kr_s, vr_s, gr_s) = _proj(
        x_sample.reshape(1, n_tok_s, D_MODEL), mod_s_stream, g_mix_l, w_in_b, cos_s, sin_s, tm=n_tok_s)

    def streams(t):
        return t.reshape(nb_s, s_new, GROUP_W)

    bucket_s = _masked_bucket(pos_s, jnp.arange(past + s_new, dtype=jnp.int32), False)
    bias_c = _bias(rel_bias, bucket_s[None, :, :past], shift=False)
    bias_n = _bias(rel_bias, bucket_s[None, :, past:], shift=False)
    oa_s = _attn_sample(lam, streams(qa_s), cache_k[l].reshape(nb_s, past * A_HEADS, HEAD_W),
                        cache_v[l].reshape(nb_s, past * A_HEADS, HEAD_W), streams(ka_s), streams(va_s),
                        bias_c, bias_n, g_sa_row, lam_init)

    or_s, ret_s = _retention(log_gamma, streams(qr_s), streams(kr_s), streams(vr_s), streams(gr_s),
                             state_ret[l].astype(F32), g_sr, c=s_new)

    def time_major(t):
        return jnp.transpose(t, (1, 0, 2)).reshape(1, n_tok_s, t.shape[-1])

    conv_prev_s = jnp.transpose(state_conv[l], (1, 0, 2)).reshape(1, (CONV_W - 1) * nb_s, 2 * D_FF)
    y_s, conv_s = _mlp(time_major(x_sample), time_major(oa_s), time_major(or_s), mod_s_time,
                       w_out_b, g_ffn_l, w_up_b, w_conv_l, b_conv_l, w_down_b, g_fin, conv_prev_s,
                       tm=n_tok_s, shift=nb_s)
    y_s = jnp.transpose(y_s.reshape(s_new, nb_s, D_MODEL), (1, 0, 2))
    conv_s = jnp.transpose(conv_s.reshape(CONV_W - 1, nb_s, 2 * D_FF), (1, 0, 2))

    def heads(t, n):
        return t.reshape(1, n, -1, A_HEADS, HEAD_W)

    return (y_p, y_s, heads(k32_p, nb_p), heads(v32_p, nb_p), ret_p[None], conv_p[None],
            heads(k32_s, nb_s), heads(v32_s, nb_s), ret_s[None], conv_s[None])
```

```python
import functools
import math

import jax
import jax.numpy as jnp
from jax import lax
from jax.experimental import pallas as pl
from jax.experimental.pallas import tpu as pltpu

F32 = jnp.float32
BF16 = jnp.bfloat16

D_MODEL = 1024
CHUNK = 64
A_HEADS = 4
A_QK_DIM = 64
HEAD_W = 128
GROUP_W = A_HEADS * HEAD_W
N_GROUPS = 7
R_HEADS = 4
D_FF = 2816
CONV_W = 3
N_BUCKETS = 32
FAR_BUCKET = N_BUCKETS // 2 - 1
MAX_DISTANCE = 128
ROPE_BASE = 10000.0
EPS = 1e-6
NEG_INF = -1e30

VMEM_LIMIT = 56 * 1024 * 1024
FF_CHUNK = 256
LOG2E = math.log2(math.e)
Q_SCALE = A_QK_DIM ** -0.5 * LOG2E
VT_ROWS = HEAD_W + 16


def _cparams(n_axes):
    return pltpu.CompilerParams(dimension_semantics=("arbitrary",) * n_axes,
                                vmem_limit_bytes=VMEM_LIMIT)


def _resident(block_shape, index_map):
    return pl.BlockSpec(block_shape, index_map, pipeline_mode=pl.Buffered(1))


def _rms(x):
    return x * lax.rsqrt(jnp.mean(x * x, axis=-1, keepdims=True) + EPS)


def _silu(x):
    return x * jax.nn.sigmoid(x)


def _dot_nt(a, b):
    return lax.dot_general(a, b, (((1,), (1,)), ((), ())), preferred_element_type=F32)


def _dot_tn(a, b):
    return lax.dot_general(a, b, (((0,), (0,)), ((), ())), preferred_element_type=F32)


def _adaln_kernel(c_ref, w_ref, b_ref, o_ref):
    a = _silu(c_ref[...]).astype(BF16)
    o_ref[...] = jnp.dot(a, w_ref[...].astype(BF16), preferred_element_type=F32) + b_ref[...]


def _adaln(c, w_ada, b_ada):
    n_rows = c.shape[0]
    n_cols = w_ada.shape[1]
    tn = 1536
    return pl.pallas_call(
        _adaln_kernel,
        out_shape=jax.ShapeDtypeStruct((n_rows, n_cols), F32),
        grid=(n_cols // tn,),
        in_specs=[pl.BlockSpec((n_rows, D_MODEL), lambda j: (0, 0)),
                  pl.BlockSpec((D_MODEL, tn), lambda j: (0, j)),
                  pl.BlockSpec((1, tn), lambda j: (0, j))],
        out_specs=pl.BlockSpec((n_rows, tn), lambda j: (0, j)),
        compiler_params=_cparams(1),
        name="adaln",
    )(c, w_ada, b_ada.reshape(1, n_cols))


def _proj_kernel(x_ref, mod_ref, g_ref, w_ref, cos_ref, sin_ref,
                 k32_ref, v32_ref, qa_ref, ka_ref, va_ref, qr_ref, kr_ref, vr_ref, gr_ref, *maybe_vt_ref,
                 tm):
    x = x_ref[0]
    h = _rms(x) * g_ref[...]
    h = h * (1.0 + mod_ref[0, 1]) + mod_ref[0, 0]
    hb = h.astype(BF16)

    def group(g):
        return jnp.dot(hb, w_ref[:, g * GROUP_W:(g + 1) * GROUP_W], preferred_element_type=F32)

    def store_heads(ref, t):
        for hh in range(A_HEADS):
            ref[0, pl.ds(hh, tm, stride=A_HEADS), :] = t[:, hh * HEAD_W:(hh + 1) * HEAD_W]

    qa_ref[0] = (group(0) * Q_SCALE).astype(BF16)
    ka = group(1)
    store_heads(k32_ref, ka)
    ka_ref[0] = ka.astype(BF16)
    va = group(2)
    store_heads(v32_ref, va)
    va_ref[0] = va.astype(BF16)
    if maybe_vt_ref:
        vt_ref = maybe_vt_ref[0]
        vt = va.T.astype(BF16)
        ones = jnp.ones((VT_ROWS - HEAD_W, tm), BF16)
        for hh in range(A_HEADS):
            vt_ref[0, hh * VT_ROWS:hh * VT_ROWS + HEAD_W, :] = vt[hh * HEAD_W:(hh + 1) * HEAD_W, :]
            vt_ref[0, hh * VT_ROWS + HEAD_W:(hh + 1) * VT_ROWS, :] = ones

    cos = cos_ref[...]
    sin = sin_ref[...]

    def rotary(t, out_ref, scale):
        for hh in range(R_HEADS):
            th = t[:, hh * HEAD_W:(hh + 1) * HEAD_W]
            r = th * cos + pltpu.roll(th, HEAD_W // 2, 1) * sin
            if scale != 1.0:
                r = r * scale
            out_ref[0, :, hh * HEAD_W:(hh + 1) * HEAD_W] = r.astype(BF16)

    rotary(group(3), qr_ref, 1.0)
    rotary(group(4), kr_ref, HEAD_W ** -0.5)
    vr_ref[0] = group(5).astype(BF16)
    gr_ref[0] = group(6).astype(BF16)


def _proj(x, mod, g_mix, w_in_b, cos_t, sin_t, tm, with_vt=False):
    b, length, _ = x.shape
    l_mod = mod.shape[2]
    mod_blk = 1 if l_mod == 1 else tm
    tok = lambda bi, i: (bi, i, 0)
    out_tok = pl.BlockSpec((1, tm, GROUP_W), tok)
    out_heads = pl.BlockSpec((1, tm * A_HEADS, HEAD_W), tok)
    heads_out = jax.ShapeDtypeStruct((b, length * A_HEADS, HEAD_W), F32)
    bf_out = jax.ShapeDtypeStruct((b, length, GROUP_W), BF16)
    out_shape = (heads_out, heads_out) + (bf_out,) * 7
    out_specs = (out_heads, out_heads) + (out_tok,) * 7
    if with_vt:
        out_shape += (jax.ShapeDtypeStruct((b, A_HEADS * VT_ROWS, length), BF16),)
        out_specs += (pl.BlockSpec((1, A_HEADS * VT_ROWS, tm), lambda bi, i: (bi, 0, i)),)
    return pl.pallas_call(
        functools.partial(_proj_kernel, tm=tm),
        out_shape=out_shape,
        grid=(b, length // tm),
        in_specs=[pl.BlockSpec((1, tm, D_MODEL), tok),
                  pl.BlockSpec((1, 6, mod_blk, D_MODEL),
                               (lambda bi, i: (bi, 0, 0, 0)) if l_mod == 1 else (lambda bi, i: (bi, 0, i, 0))),
                  pl.BlockSpec((1, D_MODEL), lambda bi, i: (0, 0)),
                  _resident((D_MODEL, N_GROUPS * GROUP_W), lambda bi, i: (0, 0)),
                  pl.BlockSpec((tm, HEAD_W), lambda bi, i: (i, 0)),
                  pl.BlockSpec((tm, HEAD_W), lambda bi, i: (i, 0))],
        out_specs=out_specs,
        compiler_params=_cparams(2),
        name="proj",
    )(x, mod, g_mix, w_in_b, cos_t, sin_t)


def _t5_bucket(rel):
    half = N_BUCKETS // 2
    max_exact = half // 2
    ret = jnp.where(rel > 0, half, 0)
    n = jnp.abs(rel)
    large = max_exact + (jnp.log(jnp.maximum(n, 1).astype(jnp.float32) / max_exact)
                         / math.log(MAX_DISTANCE / max_exact) * (half - max_exact)).astype(jnp.int32)
    large = jnp.minimum(large, half - 1)
    return ret + jnp.where(n < max_exact, n, large)


def _bias_kernel(rb_ref, bucket_ref, o_ref, *, row_blk, shift):
    def rows(r, carry):
        sl = pl.ds(pl.multiple_of(r * row_blk, row_blk), row_blk)
        b = bucket_ref[0, sl, :]
        acc = [jnp.zeros(b.shape, F32) for _ in range(A_HEADS)]
        for n in range(N_BUCKETS):
            hit = b == n
            for hh in range(A_HEADS):
                acc[hh] = jnp.where(hit, rb_ref[n * A_HEADS + hh], acc[hh])
        for hh in range(A_HEADS):
            val = acc[hh] - rb_ref[FAR_BUCKET * A_HEADS + hh] if shift else acc[hh]
            o_ref[hh, 0, sl, :] = jnp.where(b < 0, NEG_INF, val * LOG2E)
        return carry

    lax.fori_loop(0, bucket_ref.shape[1] // row_blk, rows, 0)


def _bias(rel_bias, bucket, shift):
    n, r, c = bucket.shape
    row_blk = 16 if c <= 1024 else 8
    return pl.pallas_call(
        functools.partial(_bias_kernel, row_blk=row_blk, shift=shift),
        out_shape=jax.ShapeDtypeStruct((A_HEADS, n, r, c), F32),
        grid=(n,),
        in_specs=[pl.BlockSpec(memory_space=pltpu.SMEM),
                  pl.BlockSpec((1, r, c), lambda j: (j, 0, 0))],
        out_specs=pl.BlockSpec((A_HEADS, 1, r, c), lambda j: (0, j, 0, 0)),
        compiler_params=_cparams(1),
        name="bias",
    )(rel_bias.astype(F32).reshape(N_BUCKETS * A_HEADS), bucket)


def _stack_query(q, rows):
    lane = lax.broadcasted_iota(jnp.int32, (rows, HEAD_W), 1)
    zero = jnp.zeros_like(q)
    return jnp.concatenate([jnp.where(lane < A_QK_DIM, q, zero),
                            jnp.where(lane >= A_QK_DIM, q, zero)], axis=0)


def _attn_prompt_kernel(lam_ref, q_ref, k_ref, vt_ref, bias_ref, g_ref, o_ref,
                        qs_scr, s1_scr, sg_scr, m_scr, acc_scr, *, t, group, lam_init):
    i = pl.program_id(2)
    tg = group * t
    qs_scr[...] = _stack_query(q_ref[0], t)
    acc_scr[...] = jnp.zeros_like(acc_scr)
    m_scr[...] = jnp.full_like(m_scr, -jnp.inf)

    def scores(k0, n_keys):
        k = k_ref[0, pl.ds(pl.multiple_of(k0, t), n_keys), :]
        return _dot_nt(k, qs_scr[...])

    def keep(s_scr, s, bias=None):
        if bias is not None:
            s = s + bias
        s_scr[...] = s
        return jnp.max(s, axis=0, keepdims=True)

    def fold(s_scr, k0, n_keys, s_max):
        m_prev = m_scr[...]
        m_next = jnp.maximum(m_prev, s_max)
        alpha = jnp.exp2(m_prev - m_next)
        p = jnp.exp2(s_scr[...] - m_next).astype(BF16)
        vt = vt_ref[0, :, pl.ds(pl.multiple_of(k0, t), n_keys)]
        acc_scr[...] = alpha * acc_scr[...] + jnp.dot(vt, p, preferred_element_type=F32)
        m_scr[...] = m_next

    n_near = jnp.where(i >= 1, 2 + lax.rem(i - 1, group), 1)
    n_groups = (i + 1 - n_near) // group

    def near_body(r, s_max):
        s_next = scores((i - r) * t, t)
        fold(s1_scr, (i - r + 1) * t, t, s_max)
        return keep(s1_scr, s_next, bias_ref[0, jnp.minimum(r, 2)])

    s_max = keep(s1_scr, scores(i * t, t), bias_ref[0, 0])
    s_max = lax.fori_loop(1, n_near, near_body, s_max)
    s_far = scores(0, tg)
    fold(s1_scr, (i - n_near + 1) * t, t, s_max)
    s_max = keep(sg_scr, s_far)

    def far_body(g, s_max):
        s_next = scores(g * tg, tg)
        fold(sg_scr, (g - 1) * tg, tg, s_max)
        return keep(sg_scr, s_next)

    s_max = lax.fori_loop(1, n_groups, far_body, s_max)

    @pl.when(n_groups > 0)
    def _():
        fold(sg_scr, (n_groups - 1) * tg, tg, s_max)

    acc = acc_scr[...]
    o_all = acc[:HEAD_W] / acc[HEAD_W:HEAD_W + 1]
    o = o_all[:, :t] - lam_ref[0] * o_all[:, t:]
    o = o * lax.rsqrt(jnp.mean(o * o, axis=0, keepdims=True) + EPS) * (g_ref[...] * (1.0 - lam_init))
    o_ref[0] = o.T.astype(BF16)


def _attn_prompt(lam, qa, ka, vt, bias_tiles, g_col, lam_init, t, group):
    b, s, _ = qa.shape
    kernel = functools.partial(_attn_prompt_kernel, t=t, group=group, lam_init=lam_init)
    return pl.pallas_call(
        kernel,
        out_shape=jax.ShapeDtypeStruct((b, s, GROUP_W), BF16),
        grid=(b, A_HEADS, s // t),
        in_specs=[pl.BlockSpec(memory_space=pltpu.SMEM),
                  pl.BlockSpec((1, t, HEAD_W), lambda bi, h, i: (bi, i, h)),
                  pl.BlockSpec((1, s, HEAD_W), lambda bi, h, i: (bi, 0, h)),
                  pl.BlockSpec((1, VT_ROWS, s), lambda bi, h, i: (bi, h, 0)),
                  pl.BlockSpec((1, 3, t, 2 * t), lambda bi, h, i: (h, 0, 0, 0)),
                  pl.BlockSpec((HEAD_W, 1), lambda bi, h, i: (0, 0))],
        out_specs=pl.BlockSpec((1, t, HEAD_W), lambda bi, h, i: (bi, i, h)),
        scratch_shapes=[pltpu.VMEM((2 * t, HEAD_W), BF16),
                        pltpu.VMEM((t, 2 * t), F32),
                        pltpu.VMEM((group * t, 2 * t), F32),
                        pltpu.VMEM((1, 2 * t), F32),
                        pltpu.VMEM((VT_ROWS, 2 * t), F32)],
        compiler_params=_cparams(3),
        name="attn_prompt",
    )(lam, qa, ka, vt, bias_tiles, g_col)


def _attn_sample_kernel(lam_ref, q_ref, kc_ref, vc_ref, kn_ref, vn_ref, bc_ref, bn_ref, g_ref, o_ref,
                        *, rows, past, lam_init):
    lam = lam_ref[0]
    for hh in range(A_HEADS):
        sl = slice(hh * HEAD_W, (hh + 1) * HEAD_W)
        qs = _stack_query(q_ref[0, :, sl], rows)
        kc = kc_ref[0, pl.ds(hh, past, stride=A_HEADS), :].astype(BF16)
        vc = vc_ref[0, pl.ds(hh, past, stride=A_HEADS), :].astype(BF16)
        bc = bc_ref[hh, 0]
        bn = bn_ref[hh, 0]
        s_c = _dot_nt(qs, kc) + jnp.concatenate([bc, bc], axis=0)
        s_n = _dot_nt(qs, kn_ref[0, :, sl]) + jnp.concatenate([bn, bn], axis=0)
        m = jnp.maximum(jnp.max(s_c, axis=1, keepdims=True), jnp.max(s_n, axis=1, keepdims=True))
        p_c = jnp.exp2(s_c - m)
        p_n = jnp.exp2(s_n - m)
        inv_l = 1.0 / (jnp.sum(p_c, axis=1, keepdims=True) + jnp.sum(p_n, axis=1, keepdims=True))
        p_c = p_c * inv_l
        p_n = p_n * inv_l
        d_c = (p_c[:rows] - lam * p_c[rows:]).astype(BF16)
        d_n = (p_n[:rows] - lam * p_n[rows:]).astype(BF16)
        o = (jnp.dot(d_c, vc, preferred_element_type=F32)
             + jnp.dot(d_n, vn_ref[0, :, sl], preferred_element_type=F32))
        o_ref[0, :, sl] = (_rms(o) * g_ref[...] * (1.0 - lam_init)).astype(BF16)


def _attn_sample(lam, qa, cache_k, cache_v, ka, va, bias_c, bias_n, g_row, lam_init):
    nb, rows, _ = qa.shape
    past = cache_k.shape[1] // A_HEADS
    kernel = functools.partial(_attn_sample_kernel, rows=rows, past=past, lam_init=lam_init)
    new_spec = pl.BlockSpec((1, rows, GROUP_W), lambda bi: (bi, 0, 0))
    cache_spec = pl.BlockSpec((1, past * A_HEADS, HEAD_W), lambda bi: (bi, 0, 0))
    return pl.pallas_call(
        kernel,
        out_shape=jax.ShapeDtypeStruct((nb, rows, GROUP_W), BF16),
        grid=(nb,),
        in_specs=[pl.BlockSpec(memory_space=pltpu.SMEM),
                  new_spec, cache_spec, cache_spec, new_spec, new_spec,
                  pl.BlockSpec((A_HEADS, 1, rows, past), lambda bi: (0, 0, 0, 0)),
                  pl.BlockSpec((A_HEADS, 1, rows, rows), lambda bi: (0, 0, 0, 0)),
                  pl.BlockSpec((1, HEAD_W), lambda bi: (0, 0))],
        out_specs=new_spec,
        compiler_params=_cparams(1),
        name="attn_sample",
    )(lam, qa, cache_k, cache_v, ka, va, bias_c, bias_n, g_row)


def _retention_kernel(lg_ref, q_ref, k_ref, v_ref, g_ref, s0_ref, gsub_ref, o_ref, st_ref, *, c):
    @pl.when(pl.program_id(1) == 0)
    def _():
        st_ref[...] = s0_ref[...]

    row = lax.broadcasted_iota(jnp.int32, (c, c), 0)
    col = lax.broadcasted_iota(jnp.int32, (c, c), 1)
    diff = (row - col).astype(F32)
    causal = diff >= 0.0
    diff = jnp.maximum(diff, 0.0)
    idx = lax.broadcasted_iota(jnp.int32, (c, 1), 0).astype(F32)
    gsub = gsub_ref[...]

    for hh in range(R_HEADS):
        lg = lg_ref[hh]
        sl = slice(hh * HEAD_W, (hh + 1) * HEAD_W)
        q = q_ref[0, :, sl]
        k = k_ref[0, :, sl]
        v = v_ref[0, :, sl]
        decay = jnp.where(causal, jnp.exp(lg * diff), 0.0)
        scores = _dot_nt(q, k) * decay
        state = st_ref[0, hh]
        o = jnp.dot(scores.astype(BF16), v, preferred_element_type=F32)
        o = o + jnp.dot(q, state.astype(BF16), preferred_element_type=F32) * jnp.exp(lg * (idx + 1.0))
        kd = (k.astype(F32) * jnp.exp(lg * (c - 1.0 - idx))).astype(BF16)
        st_ref[0, hh] = jnp.exp(lg * c) * state + _dot_tn(kd, v)
        gate = g_ref[0, :, sl].astype(F32)
        o_ref[0, :, sl] = (_rms(o) * gsub * _silu(gate)).astype(BF16)


def _retention(log_gamma, qr, kr, vr, gr, state0, g_sub_r, c):
    b, s, _ = qr.shape
    kernel = functools.partial(_retention_kernel, c=c)
    tok = pl.BlockSpec((1, c, GROUP_W), lambda bi, i: (bi, i, 0))
    st = pl.BlockSpec((1, R_HEADS, HEAD_W, HEAD_W), lambda bi, i: (bi, 0, 0, 0))
    return pl.pallas_call(
        kernel,
        out_shape=(jax.ShapeDtypeStruct((b, s, GROUP_W), BF16),
                   jax.ShapeDtypeStruct((b, R_HEADS, HEAD_W, HEAD_W), F32)),
        grid=(b, s // c),
        in_specs=[pl.BlockSpec(memory_space=pltpu.SMEM), tok, tok, tok, tok, st,
                  pl.BlockSpec((1, HEAD_W), lambda bi, i: (0, 0))],
        out_specs=(tok, st),
        compiler_params=_cparams(2),
        name="retention",
    )(log_gamma, qr, kr, vr, gr, state0, g_sub_r)


def _mlp_kernel(x_ref, oa_ref, or_ref, mod_ref, wout_ref, gffn_ref, wup_ref, wconv_ref, bconv_ref,
                wdown_ref, gfin_ref, cprev_ref, y_ref, cst_ref, ue_scr, acc_scr, *, tm, shift):
    halo = (CONV_W - 1) * shift
    pad = -(-halo // 8) * 8

    @pl.when(pl.program_id(1) == 0)
    def _():
        cst_ref[...] = cprev_ref[...]

    mixed = (jnp.dot(oa_ref[0], wout_ref[:GROUP_W, :], preferred_element_type=F32)
             + jnp.dot(or_ref[0], wout_ref[GROUP_W:, :], preferred_element_type=F32))
    x1 = x_ref[0] + mod_ref[0, 2] * mixed
    h = _rms(x1) * gffn_ref[...]
    h = h * (1.0 + mod_ref[0, 4]) + mod_ref[0, 3]
    hb = h.astype(BF16)

    def conv_half(slot, col0):
        cols = slice(col0, col0 + FF_CHUNK)
        u = jnp.dot(hb, wup_ref[:, cols], preferred_element_type=F32)
        ue_scr[slot, pad - halo:pad, :] = cst_ref[0, :, cols]
        ue_scr[slot, pad:pad + tm, :] = u
        cst_ref[0, :, cols] = u[tm - halo:, :]
        y = wconv_ref[CONV_W - 1:CONV_W, cols] * u + bconv_ref[:, cols]
        for j in range(CONV_W - 1):
            start = pad - halo + j * shift
            y = y + wconv_ref[j:j + 1, cols] * ue_scr[slot, start:start + tm, :]
        return y

    for ch in range(D_FF // FF_CHUNK):
        ya = conv_half(0, ch * FF_CHUNK)
        yg = conv_half(1, D_FF + ch * FF_CHUNK)
        act = (_silu(ya) * yg).astype(BF16)
        part = jnp.dot(act, wdown_ref[ch * FF_CHUNK:(ch + 1) * FF_CHUNK, :], preferred_element_type=F32)
        if ch == 0:
            acc_scr[...] = part
        else:
            acc_scr[...] += part

    x2 = x1 + mod_ref[0, 5] * acc_scr[...]
    y_ref[0] = _rms(x2) * gfin_ref[...]


def _mlp(x, oa, orr, mod, w_out_b, g_ffn, w_up_b, w_conv, b_conv, w_down_b, g_final, conv_prev, tm, shift):
    b, length, _ = x.shape
    l_mod = mod.shape[2]
    mod_blk = 1 if l_mod == 1 else tm
    halo = (CONV_W - 1) * shift
    pad = -(-halo // 8) * 8
    kernel = functools.partial(_mlp_kernel, tm=tm, shift=shift)
    tok = lambda bi, i: (bi, i, 0)
    const = lambda bi, i: (0, 0)
    return pl.pallas_call(
        kernel,
        out_shape=(jax.ShapeDtypeStruct((b, length, D_MODEL), F32),
                   jax.ShapeDtypeStruct((b, halo, 2 * D_FF), F32)),
        grid=(b, length // tm),
        in_specs=[pl.BlockSpec((1, tm, D_MODEL), tok),
                  pl.BlockSpec((1, tm, GROUP_W), tok),
                  pl.BlockSpec((1, tm, GROUP_W), tok),
                  pl.BlockSpec((1, 6, mod_blk, D_MODEL),
                               (lambda bi, i: (bi, 0, 0, 0)) if l_mod == 1 else (lambda bi, i: (bi, 0, i, 0))),
                  _resident((2 * GROUP_W, D_MODEL), const),
                  pl.BlockSpec((1, D_MODEL), const),
                  _resident((D_MODEL, 2 * D_FF), const),
                  pl.BlockSpec((CONV_W, 2 * D_FF), const),
                  pl.BlockSpec((1, 2 * D_FF), const),
                  _resident((D_FF, D_MODEL), const),
                  pl.BlockSpec((1, D_MODEL), const),
                  pl.BlockSpec((1, halo, 2 * D_FF), lambda bi, i: (bi, 0, 0))],
        out_specs=(pl.BlockSpec((1, tm, D_MODEL), tok),
                   pl.BlockSpec((1, halo, 2 * D_FF), lambda bi, i: (bi, 0, 0))),
        scratch_shapes=[pltpu.VMEM((2, pad + tm, FF_CHUNK), F32),
                        pltpu.VMEM((tm, D_MODEL), F32)],
        compiler_params=_cparams(2),
        name="mlp",
    )(x, oa, orr, mod, w_out_b, g_ffn, w_up_b, w_conv, b_conv, w_down_b, g_final, conv_prev)


def _rotary_tables(pos):
    half = HEAD_W // 2
    inv_freq = ROPE_BASE ** (-jnp.arange(half, dtype=F32) / half)
    ang = pos.astype(F32)[:, None] * inv_freq[None, :]
    cos = jnp.cos(ang)
    sin = jnp.sin(ang)
    return jnp.concatenate([cos, cos], axis=-1), jnp.concatenate([-sin, sin], axis=-1)


def _masked_bucket(q_pos, k_pos, keys_major):
    rel = k_pos[None, :] - q_pos[:, None]
    visible = (k_pos[None, :] // CHUNK) <= (q_pos[:, None] // CHUNK)
    bucket = jnp.where(visible, _t5_bucket(rel), -1).astype(jnp.int32)
    return bucket.T if keys_major else bucket


def kernel(x_prompt, x_sample, cache_k, cache_v, state_ret, state_conv, c_prompt, c_sample,
           w_ada, b_ada, g_mix, w_in, lambda_q1, lambda_k1, lambda_q2, lambda_k2,
           g_sub_a, g_sub_r, w_out, g_ffn, w_up, w_conv, b_conv, w_down, rel_bias, g_final):
    depth = w_in.shape[0]
    assert depth == 1
    l = 0
    nb_p, s_p, _ = x_prompt.shape
    nb_s, s_new, _ = x_sample.shape
    past = cache_k.shape[2]
    n_tok_s = nb_s * s_new

    log_gamma = jnp.log(1.0 - 2.0 ** (-5.0 - jnp.arange(R_HEADS, dtype=F32)))
    lam_init = 0.8 - 0.6 * math.exp(-0.3 * l)
    lam = (jnp.exp(jnp.sum(lambda_q1[l].astype(F32) * lambda_k1[l].astype(F32)))
           - jnp.exp(jnp.sum(lambda_q2[l].astype(F32) * lambda_k2[l].astype(F32))) + lam_init).reshape(1)

    w_in_b = w_in[l].astype(BF16)
    w_out_b = w_out[l].astype(BF16)
    w_up_b = w_up[l].astype(BF16)
    w_down_b = w_down[l].astype(BF16)
    g_mix_l = g_mix[l].reshape(1, D_MODEL)
    g_ffn_l = g_ffn[l].reshape(1, D_MODEL)
    g_fin = g_final.reshape(1, D_MODEL)
    g_sa_row = g_sub_a[l].reshape(1, HEAD_W)
    g_sa_col = g_sub_a[l].reshape(HEAD_W, 1)
    g_sr = g_sub_r[l].reshape(1, HEAD_W)
    w_conv_l = w_conv[l]
    b_conv_l = b_conv[l].reshape(1, 2 * D_FF)

    mods = _adaln(jnp.concatenate([c_prompt, c_sample], axis=0), w_ada[l], b_ada[l])
    mods = mods.reshape(nb_p + nb_s, 6, D_MODEL)
    mod_p = mods[:nb_p].reshape(nb_p, 6, 1, D_MODEL)
    mod_s = jnp.transpose(mods[nb_p:], (1, 0, 2))
    mod_s_stream = jnp.repeat(mod_s, s_new, axis=1)[None]
    mod_s_time = jnp.tile(mod_s, (1, s_new, 1))[None]

    t_attn = 256
    cos_p, sin_p = _rotary_tables(jnp.arange(s_p, dtype=jnp.int32))
    (k32_p, v32_p, qa_p, ka_p, _, qr_p, kr_p, vr_p, gr_p, vt_p) = _proj(
        x_prompt, mod_p, g_mix_l, w_in_b, cos_p, sin_p, tm=512, with_vt=True)

    tile_pos = jnp.arange(t_attn, dtype=jnp.int32) + t_attn
    bucket_tiles = jnp.stack([_masked_bucket(tile_pos, tile_pos, True),
                              _masked_bucket(tile_pos, tile_pos - t_attn, True),
                              jnp.full((t_attn, t_attn), FAR_BUCKET, jnp.int32)])
    bias_tiles = _bias(rel_bias, jnp.tile(bucket_tiles, (1, 1, 2)), shift=True)
    oa_p = _attn_prompt(lam, qa_p, ka_p, vt_p, bias_tiles, g_sa_col, lam_init, t_attn, group=4)

    or_p, ret_p = _retention(log_gamma, qr_p, kr_p, vr_p, gr_p,
                             jnp.zeros((nb_p, R_HEADS, HEAD_W, HEAD_W), F32), g_sr, c=256)

    y_p, conv_p = _mlp(x_prompt, oa_p, or_p, mod_p, w_out_b, g_ffn_l, w_up_b, w_conv_l, b_conv_l,
                       w_down_b, g_fin, jnp.zeros((nb_p, CONV_W - 1, 2 * D_FF), F32), tm=512, shift=1)

    pos_s = past + jnp.arange(s_new, dtype=jnp.int32)
    cos_s, sin_s = _rotary_tables(pos_s)
    cos_s = jnp.tile(cos_s, (nb_s, 1))
    sin_s = jnp.tile(sin_s, (nb_s, 1))
    (k32_s, v32_s, qa_s, ka_s, va_s, qr_s, kr_s, vr_s, gr_s) = _proj(
        x_sample.reshape(1, n_tok_s, D_MODEL), mod_s_stream, g_mix_l, w_in_b, cos_s, sin_s, tm=n_tok_s)

    def streams(t):
        return t.reshape(nb_s, s_new, GROUP_W)

    bucket_s = _masked_bucket(pos_s, jnp.arange(past + s_new, dtype=jnp.int32), False)
    bias_c = _bias(rel_bias, bucket_s[None, :, :past], shift=False)
    bias_n = _bias(rel_bias, bucket_s[None, :, past:], shift=False)
    oa_s = _attn_sample(lam, streams(qa_s), cache_k[l].reshape(nb_s, past * A_HEADS, HEAD_W),
                        cache_v[l].reshape(nb_s, past * A_HEADS, HEAD_W), streams(ka_s), streams(va_s),
                        bias_c, bias_n, g_sa_row, lam_init)

    or_s, ret_s = _retention(log_gamma, streams(qr_s), streams(kr_s), streams(vr_s), streams(gr_s),
                             state_ret[l].astype(F32), g_sr, c=s_new)

    def time_major(t):
        return jnp.transpose(t, (1, 0, 2)).reshape(1, n_tok_s, t.shape[-1])

    conv_prev_s = jnp.transpose(state_conv[l], (1, 0, 2)).reshape(1, (CONV_W - 1) * nb_s, 2 * D_FF)
    y_s, conv_s = _mlp(time_major(x_sample), time_major(oa_s), time_major(or_s), mod_s_time,
                       w_out_b, g_ffn_l, w_up_b, w_conv_l, b_conv_l, w_down_b, g_fin, conv_prev_s,
                       tm=n_tok_s, shift=nb_s)
    y_s = jnp.transpose(y_s.reshape(s_new, nb_s, D_MODEL), (1, 0, 2))
    conv_s = jnp.transpose(conv_s.reshape(CONV_W - 1, nb_s, 2 * D_FF), (1, 0, 2))

    def heads(t, n):
        return t.reshape(1, n, -1, A_HEADS, HEAD_W)

    return (y_p, y_s, heads(k32_p, nb_p), heads(v32_p, nb_p), ret_p[None], conv_p[None],
            heads(k32_s, nb_s), heads(v32_s, nb_s), ret_s[None], conv_s[None])
```

```python
import functools
import math

import jax
import jax.numpy as jnp
from jax import lax
from jax.experimental import pallas as pl
from jax.experimental.pallas import tpu as pltpu

F32 = jnp.float32
BF16 = jnp.bfloat16

D_MODEL = 1024
CHUNK = 64
A_HEADS = 4
A_QK_DIM = 64
HEAD_W = 128
GROUP_W = A_HEADS * HEAD_W
N_GROUPS = 7
R_HEADS = 4
D_FF = 2816
CONV_W = 3
N_BUCKETS = 32
FAR_BUCKET = N_BUCKETS // 2 - 1
MAX_DISTANCE = 128
ROPE_BASE = 10000.0
EPS = 1e-6
NEG_INF = -1e30

VMEM_LIMIT = 56 * 1024 * 1024
FF_CHUNK = 256
LOG2E = math.log2(math.e)
Q_SCALE = A_QK_DIM ** -0.5 * LOG2E
VT_ROWS = HEAD_W + 16


def _cparams(n_axes, flags=None):
    return pltpu.CompilerParams(dimension_semantics=("arbitrary",) * n_axes,
                                vmem_limit_bytes=VMEM_LIMIT, flags=flags)


def _resident(block_shape, index_map):
    return pl.BlockSpec(block_shape, index_map, pipeline_mode=pl.Buffered(1))


def _rms(x):
    return x * lax.rsqrt(jnp.mean(x * x, axis=-1, keepdims=True) + EPS)


def _silu(x):
    return x * jax.nn.sigmoid(x)


def _dot_nt(a, b):
    return lax.dot_general(a, b, (((1,), (1,)), ((), ())), preferred_element_type=F32)


def _dot_tn(a, b):
    return lax.dot_general(a, b, (((0,), (0,)), ((), ())), preferred_element_type=F32)


def _adaln_kernel(c_ref, w_ref, b_ref, o_ref):
    a = _silu(c_ref[...]).astype(BF16)
    o_ref[...] = jnp.dot(a, w_ref[...].astype(BF16), preferred_element_type=F32) + b_ref[...]


def _adaln(c, w_ada, b_ada):
    n_rows = c.shape[0]
    n_cols = w_ada.shape[1]
    tn = 1536
    return pl.pallas_call(
        _adaln_kernel,
        out_shape=jax.ShapeDtypeStruct((n_rows, n_cols), F32),
        grid=(n_cols // tn,),
        in_specs=[pl.BlockSpec((n_rows, D_MODEL), lambda j: (0, 0)),
                  pl.BlockSpec((D_MODEL, tn), lambda j: (0, j)),
                  pl.BlockSpec((1, tn), lambda j: (0, j))],
        out_specs=pl.BlockSpec((n_rows, tn), lambda j: (0, j)),
        compiler_params=_cparams(1),
        name="adaln",
    )(c, w_ada, b_ada.reshape(1, n_cols))


def _proj_kernel(x_ref, mod_ref, g_ref, w_ref, cos_ref, sin_ref,
                 k32_ref, v32_ref, qa_ref, ka_ref, va_ref, qr_ref, kr_ref, vr_ref, gr_ref, *maybe_vt_ref,
                 tm):
    x = x_ref[0]
    h = _rms(x) * g_ref[...]
    h = h * (1.0 + mod_ref[0, 1]) + mod_ref[0, 0]
    hb = h.astype(BF16)

    def group(g):
        return jnp.dot(hb, w_ref[:, g * GROUP_W:(g + 1) * GROUP_W], preferred_element_type=F32)

    def store_heads(ref, t):
        for hh in range(A_HEADS):
            ref[0, pl.ds(hh, tm, stride=A_HEADS), :] = t[:, hh * HEAD_W:(hh + 1) * HEAD_W]

    qa_ref[0] = (group(0) * Q_SCALE).astype(BF16)
    ka = group(1)
    store_heads(k32_ref, ka)
    ka_ref[0] = ka.astype(BF16)
    va = group(2)
    store_heads(v32_ref, va)
    va_ref[0] = va.astype(BF16)
    if maybe_vt_ref:
        vt_ref = maybe_vt_ref[0]
        vt = va.T.astype(BF16)
        ones = jnp.ones((VT_ROWS - HEAD_W, tm), BF16)
        for hh in range(A_HEADS):
            vt_ref[0, hh * VT_ROWS:hh * VT_ROWS + HEAD_W, :] = vt[hh * HEAD_W:(hh + 1) * HEAD_W, :]
            vt_ref[0, hh * VT_ROWS + HEAD_W:(hh + 1) * VT_ROWS, :] = ones

    cos = cos_ref[...]
    sin = sin_ref[...]

    def rotary(t, out_ref, scale):
        for hh in range(R_HEADS):
            th = t[:, hh * HEAD_W:(hh + 1) * HEAD_W]
            r = th * cos + pltpu.roll(th, HEAD_W // 2, 1) * sin
            if scale != 1.0:
                r = r * scale
            out_ref[0, :, hh * HEAD_W:(hh + 1) * HEAD_W] = r.astype(BF16)

    rotary(group(3), qr_ref, 1.0)
    rotary(group(4), kr_ref, HEAD_W ** -0.5)
    vr_ref[0] = group(5).astype(BF16)
    gr_ref[0] = group(6).astype(BF16)


def _proj(x, mod, g_mix, w_in_b, cos_t, sin_t, tm, with_vt=False):
    b, length, _ = x.shape
    l_mod = mod.shape[2]
    mod_blk = 1 if l_mod == 1 else tm
    tok = lambda bi, i: (bi, i, 0)
    out_tok = pl.BlockSpec((1, tm, GROUP_W), tok)
    out_heads = pl.BlockSpec((1, tm * A_HEADS, HEAD_W), tok)
    heads_out = jax.ShapeDtypeStruct((b, length * A_HEADS, HEAD_W), F32)
    bf_out = jax.ShapeDtypeStruct((b, length, GROUP_W), BF16)
    out_shape = (heads_out, heads_out) + (bf_out,) * 7
    out_specs = (out_heads, out_heads) + (out_tok,) * 7
    if with_vt:
        out_shape += (jax.ShapeDtypeStruct((b, A_HEADS * VT_ROWS, length), BF16),)
        out_specs += (pl.BlockSpec((1, A_HEADS * VT_ROWS, tm), lambda bi, i: (bi, 0, i)),)
    return pl.pallas_call(
        functools.partial(_proj_kernel, tm=tm),
        out_shape=out_shape,
        grid=(b, length // tm),
        in_specs=[pl.BlockSpec((1, tm, D_MODEL), tok),
                  pl.BlockSpec((1, 6, mod_blk, D_MODEL),
                               (lambda bi, i: (bi, 0, 0, 0)) if l_mod == 1 else (lambda bi, i: (bi, 0, i, 0))),
                  pl.BlockSpec((1, D_MODEL), lambda bi, i: (0, 0)),
                  _resident((D_MODEL, N_GROUPS * GROUP_W), lambda bi, i: (0, 0)),
                  pl.BlockSpec((tm, HEAD_W), lambda bi, i: (i, 0)),
                  pl.BlockSpec((tm, HEAD_W), lambda bi, i: (i, 0))],
        out_specs=out_specs,
        compiler_params=_cparams(2),
        name="proj",
    )(x, mod, g_mix, w_in_b, cos_t, sin_t)


def _t5_bucket(rel):
    half = N_BUCKETS // 2
    max_exact = half // 2
    ret = jnp.where(rel > 0, half, 0)
    n = jnp.abs(rel)
    large = max_exact + (jnp.log(jnp.maximum(n, 1).astype(jnp.float32) / max_exact)
                         / math.log(MAX_DISTANCE / max_exact) * (half - max_exact)).astype(jnp.int32)
    large = jnp.minimum(large, half - 1)
    return ret + jnp.where(n < max_exact, n, large)


def _bias_kernel(rb_ref, bucket_ref, o_ref, *, row_blk, shift):
    def rows(r, carry):
        sl = pl.ds(pl.multiple_of(r * row_blk, row_blk), row_blk)
        b = bucket_ref[0, sl, :]
        acc = [jnp.zeros(b.shape, F32) for _ in range(A_HEADS)]
        for n in range(N_BUCKETS):
            hit = b == n
            for hh in range(A_HEADS):
                acc[hh] = jnp.where(hit, rb_ref[n * A_HEADS + hh], acc[hh])
        for hh in range(A_HEADS):
            val = acc[hh] - rb_ref[FAR_BUCKET * A_HEADS + hh] if shift else acc[hh]
            o_ref[hh, 0, sl, :] = jnp.where(b < 0, NEG_INF, val * LOG2E)
        return carry

    lax.fori_loop(0, bucket_ref.shape[1] // row_blk, rows, 0)


def _bias(rel_bias, bucket, shift):
    n, r, c = bucket.shape
    row_blk = 16 if c <= 1024 else 8
    return pl.pallas_call(
        functools.partial(_bias_kernel, row_blk=row_blk, shift=shift),
        out_shape=jax.ShapeDtypeStruct((A_HEADS, n, r, c), F32),
        grid=(n,),
        in_specs=[pl.BlockSpec(memory_space=pltpu.SMEM),
                  pl.BlockSpec((1, r, c), lambda j: (j, 0, 0))],
        out_specs=pl.BlockSpec((A_HEADS, 1, r, c), lambda j: (0, j, 0, 0)),
        compiler_params=_cparams(1),
        name="bias",
    )(rel_bias.astype(F32).reshape(N_BUCKETS * A_HEADS), bucket)


def _stack_query(q, rows):
    lane = lax.broadcasted_iota(jnp.int32, (rows, HEAD_W), 1)
    zero = jnp.zeros_like(q)
    return jnp.concatenate([jnp.where(lane < A_QK_DIM, q, zero),
                            jnp.where(lane >= A_QK_DIM, q, zero)], axis=0)


def _attn_prompt_kernel(lam_ref, q_ref, k_ref, vt_ref, bias_ref, g_ref, o_ref,
                        qs_scr, s1_scr, sg_scr, m_scr, acc_scr, *, t, group, lam_init):
    i = pl.program_id(2)
    tg = group * t
    qs_scr[...] = _stack_query(q_ref[0], t)
    acc_scr[...] = jnp.zeros_like(acc_scr)
    m_scr[...] = jnp.full_like(m_scr, -jnp.inf)

    def scores(k0, n_keys):
        k = k_ref[0, pl.ds(pl.multiple_of(k0, t), n_keys), :]
        return _dot_nt(k, qs_scr[...])

    def keep(s_scr, s, bias=None):
        if bias is not None:
            s = s + bias
        s_scr[...] = s
        return jnp.max(s, axis=0, keepdims=True)

    def fold(s_scr, k0, n_keys, s_max):
        m_prev = m_scr[...]
        m_next = jnp.maximum(m_prev, s_max)
        alpha = jnp.exp2(m_prev - m_next)
        p = jnp.exp2(s_scr[...] - m_next).astype(BF16)
        vt = vt_ref[0, :, pl.ds(pl.multiple_of(k0, t), n_keys)]
        acc_scr[...] = alpha * acc_scr[...] + jnp.dot(vt, p, preferred_element_type=F32)
        m_scr[...] = m_next

    n_near = jnp.where(i >= 1, 2 + lax.rem(i - 1, group), 1)
    n_groups = (i + 1 - n_near) // group

    def near_body(r, s_max):
        s_next = scores((i - r) * t, t)
        fold(s1_scr, (i - r + 1) * t, t, s_max)
        return keep(s1_scr, s_next, bias_ref[0, jnp.minimum(r, 2)])

    s_max = keep(s1_scr, scores(i * t, t), bias_ref[0, 0])
    s_max = lax.fori_loop(1, n_near, near_body, s_max)
    s_far = scores(0, tg)
    fold(s1_scr, (i - n_near + 1) * t, t, s_max)
    s_max = keep(sg_scr, s_far)

    def far_body(g, s_max):
        s_next = scores(g * tg, tg)
        fold(sg_scr, (g - 1) * tg, tg, s_max)
        return keep(sg_scr, s_next)

    s_max = lax.fori_loop(1, n_groups, far_body, s_max)

    @pl.when(n_groups > 0)
    def _():
        fold(sg_scr, (n_groups - 1) * tg, tg, s_max)

    acc = acc_scr[...]
    o_all = acc[:HEAD_W] / acc[HEAD_W:HEAD_W + 1]
    o = o_all[:, :t] - lam_ref[0] * o_all[:, t:]
    o = o * lax.rsqrt(jnp.mean(o * o, axis=0, keepdims=True) + EPS) * (g_ref[...] * (1.0 - lam_init))
    o_ref[0] = o.T.astype(BF16)


def _attn_prompt(lam, qa, ka, vt, bias_tiles, g_col, lam_init, t, group):
    b, s, _ = qa.shape
    kernel = functools.partial(_attn_prompt_kernel, t=t, group=group, lam_init=lam_init)
    return pl.pallas_call(
        kernel,
        out_shape=jax.ShapeDtypeStruct((b, s, GROUP_W), BF16),
        grid=(b, A_HEADS, s // t),
        in_specs=[pl.BlockSpec(memory_space=pltpu.SMEM),
                  pl.BlockSpec((1, t, HEAD_W), lambda bi, h, i: (bi, i, h)),
                  pl.BlockSpec((1, s, HEAD_W), lambda bi, h, i: (bi, 0, h)),
                  pl.BlockSpec((1, VT_ROWS, s), lambda bi, h, i: (bi, h, 0)),
                  pl.BlockSpec((1, 3, t, 2 * t), lambda bi, h, i: (h, 0, 0, 0)),
                  pl.BlockSpec((HEAD_W, 1), lambda bi, h, i: (0, 0))],
        out_specs=pl.BlockSpec((1, t, HEAD_W), lambda bi, h, i: (bi, i, h)),
        scratch_shapes=[pltpu.VMEM((2 * t, HEAD_W), BF16),
                        pltpu.VMEM((t, 2 * t), F32),
                        pltpu.VMEM((group * t, 2 * t), F32),
                        pltpu.VMEM((1, 2 * t), F32),
                        pltpu.VMEM((VT_ROWS, 2 * t), F32)],
        compiler_params=_cparams(3),
        name="attn_prompt",
    )(lam, qa, ka, vt, bias_tiles, g_col)


def _attn_sample_kernel(lam_ref, q_ref, kc_ref, vc_ref, kn_ref, vn_ref, bc_ref, bn_ref, g_ref, o_ref,
                        *, rows, past, lam_init):
    lam = lam_ref[0]
    for hh in range(A_HEADS):
        sl = slice(hh * HEAD_W, (hh + 1) * HEAD_W)
        qs = _stack_query(q_ref[0, :, sl], rows)
        kc = kc_ref[0, pl.ds(hh, past, stride=A_HEADS), :].astype(BF16)
        vc = vc_ref[0, pl.ds(hh, past, stride=A_HEADS), :].astype(BF16)
        bc = bc_ref[hh, 0]
        bn = bn_ref[hh, 0]
        s_c = _dot_nt(qs, kc) + jnp.concatenate([bc, bc], axis=0)
        s_n = _dot_nt(qs, kn_ref[0, :, sl]) + jnp.concatenate([bn, bn], axis=0)
        m = jnp.maximum(jnp.max(s_c, axis=1, keepdims=True), jnp.max(s_n, axis=1, keepdims=True))
        p_c = jnp.exp2(s_c - m)
        p_n = jnp.exp2(s_n - m)
        inv_l = 1.0 / (jnp.sum(p_c, axis=1, keepdims=True) + jnp.sum(p_n, axis=1, keepdims=True))
        p_c = p_c * inv_l
        p_n = p_n * inv_l
        d_c = (p_c[:rows] - lam * p_c[rows:]).astype(BF16)
        d_n = (p_n[:rows] - lam * p_n[rows:]).astype(BF16)
        o = (jnp.dot(d_c, vc, preferred_element_type=F32)
             + jnp.dot(d_n, vn_ref[0, :, sl], preferred_element_type=F32))
        o_ref[0, :, sl] = (_rms(o) * g_ref[...] * (1.0 - lam_init)).astype(BF16)


def _attn_sample(lam, qa, cache_k, cache_v, ka, va, bias_c, bias_n, g_row, lam_init):
    nb, rows, _ = qa.shape
    past = cache_k.shape[1] // A_HEADS
    kernel = functools.partial(_attn_sample_kernel, rows=rows, past=past, lam_init=lam_init)
    new_spec = pl.BlockSpec((1, rows, GROUP_W), lambda bi: (bi, 0, 0))
    cache_spec = pl.BlockSpec((1, past * A_HEADS, HEAD_W), lambda bi: (bi, 0, 0))
    return pl.pallas_call(
        kernel,
        out_shape=jax.ShapeDtypeStruct((nb, rows, GROUP_W), BF16),
        grid=(nb,),
        in_specs=[pl.BlockSpec(memory_space=pltpu.SMEM),
                  new_spec, cache_spec, cache_spec, new_spec, new_spec,
                  pl.BlockSpec((A_HEADS, 1, rows, past), lambda bi: (0, 0, 0, 0)),
                  pl.BlockSpec((A_HEADS, 1, rows, rows), lambda bi: (0, 0, 0, 0)),
                  pl.BlockSpec((1, HEAD_W), lambda bi: (0, 0))],
        out_specs=new_spec,
        compiler_params=_cparams(1),
        name="attn_sample",
    )(lam, qa, cache_k, cache_v, ka, va, bias_c, bias_n, g_row)


def _retention_kernel(lg_ref, q_ref, k_ref, v_ref, g_ref, s0_ref, gsub_ref, o_ref, st_ref, *, c):
    @pl.when(pl.program_id(1) == 0)
    def _():
        st_ref[...] = s0_ref[...]

    row = lax.broadcasted_iota(jnp.int32, (c, c), 0)
    col = lax.broadcasted_iota(jnp.int32, (c, c), 1)
    diff = (row - col).astype(F32)
    causal = diff >= 0.0
    diff = jnp.maximum(diff, 0.0)
    idx = lax.broadcasted_iota(jnp.int32, (c, 1), 0).astype(F32)
    gsub = gsub_ref[...]

    for hh in range(R_HEADS):
        lg = lg_ref[hh]
        sl = slice(hh * HEAD_W, (hh + 1) * HEAD_W)
        q = q_ref[0, :, sl]
        k = k_ref[0, :, sl]
        v = v_ref[0, :, sl]
        decay = jnp.where(causal, jnp.exp(lg * diff), 0.0)
        scores = _dot_nt(q, k) * decay
        state = st_ref[0, hh]
        o = jnp.dot(scores.astype(BF16), v, preferred_element_type=F32)
        o = o + jnp.dot(q, state.astype(BF16), preferred_element_type=F32) * jnp.exp(lg * (idx + 1.0))
        kd = (k.astype(F32) * jnp.exp(lg * (c - 1.0 - idx))).astype(BF16)
        st_ref[0, hh] = jnp.exp(lg * c) * state + _dot_tn(kd, v)
        gate = g_ref[0, :, sl].astype(F32)
        o_ref[0, :, sl] = (_rms(o) * gsub * _silu(gate)).astype(BF16)


def _retention(log_gamma, qr, kr, vr, gr, state0, g_sub_r, c):
    b, s, _ = qr.shape
    kernel = functools.partial(_retention_kernel, c=c)
    tok = pl.BlockSpec((1, c, GROUP_W), lambda bi, i: (bi, i, 0))
    st = pl.BlockSpec((1, R_HEADS, HEAD_W, HEAD_W), lambda bi, i: (bi, 0, 0, 0))
    return pl.pallas_call(
        kernel,
        out_shape=(jax.ShapeDtypeStruct((b, s, GROUP_W), BF16),
                   jax.ShapeDtypeStruct((b, R_HEADS, HEAD_W, HEAD_W), F32)),
        grid=(b, s // c),
        in_specs=[pl.BlockSpec(memory_space=pltpu.SMEM), tok, tok, tok, tok, st,
                  pl.BlockSpec((1, HEAD_W), lambda bi, i: (0, 0))],
        out_specs=(tok, st),
        compiler_params=_cparams(2),
        name="retention",
    )(log_gamma, qr, kr, vr, gr, state0, g_sub_r)


def _mlp_kernel(x_ref, oa_ref, or_ref, mod_ref, wout_ref, gffn_ref, wup_ref, wconv_ref, bconv_ref,
                wdown_ref, gfin_ref, cprev_ref, y_ref, cst_ref, ue_scr, acc_scr, hb_scr, *, tm, shift):
    halo = (CONV_W - 1) * shift
    pad = -(-halo // 8) * 8

    @pl.when(pl.program_id(1) == 0)
    def _():
        cst_ref[...] = cprev_ref[...]

    mixed = (jnp.dot(oa_ref[0], wout_ref[:GROUP_W, :], preferred_element_type=F32)
             + jnp.dot(or_ref[0], wout_ref[GROUP_W:, :], preferred_element_type=F32))
    x1 = x_ref[0] + mod_ref[0, 2] * mixed
    y_ref[0] = x1
    h = _rms(x1) * gffn_ref[...]
    h = h * (1.0 + mod_ref[0, 4]) + mod_ref[0, 3]
    hb_scr[...] = h.astype(BF16)

    def up_half(slot, col0):
        cols = slice(col0, col0 + FF_CHUNK)
        ue_scr[slot, pad - halo:pad, :] = cst_ref[0, :, cols]
        ue_scr[slot, pad:pad + tm, :] = jnp.dot(hb_scr[...], wup_ref[:, cols], preferred_element_type=F32)
        cst_ref[0, :, cols] = ue_scr[slot, pad + tm - halo:pad + tm, :]

    def up(ch):
        up_half(2 * (ch % 2), ch * FF_CHUNK)
        up_half(2 * (ch % 2) + 1, D_FF + ch * FF_CHUNK)

    def conv_half(slot, col0):
        cols = slice(col0, col0 + FF_CHUNK)
        y = bconv_ref[:, cols]
        for j in range(CONV_W):
            start = pad - halo + j * shift
            y = y + wconv_ref[j:j + 1, cols] * ue_scr[slot, start:start + tm, :]
        return y

    n_chunks = D_FF // FF_CHUNK
    up(0)
    for ch in range(n_chunks):
        if ch + 1 < n_chunks:
            up(ch + 1)
        ya = conv_half(2 * (ch % 2), ch * FF_CHUNK)
        yg = conv_half(2 * (ch % 2) + 1, D_FF + ch * FF_CHUNK)
        act = (_silu(ya) * yg).astype(BF16)
        part = jnp.dot(act, wdown_ref[ch * FF_CHUNK:(ch + 1) * FF_CHUNK, :], preferred_element_type=F32)
        if ch == 0:
            acc_scr[...] = part
        else:
            acc_scr[...] += part

    x2 = y_ref[0] + mod_ref[0, 5] * acc_scr[...]
    y_ref[0] = _rms(x2) * gfin_ref[...]


def _mlp(x, oa, orr, mod, w_out_b, g_ffn, w_up_b, w_conv, b_conv, w_down_b, g_final, conv_prev, tm, shift):
    b, length, _ = x.shape
    l_mod = mod.shape[2]
    mod_blk = 1 if l_mod == 1 else tm
    halo = (CONV_W - 1) * shift
    pad = -(-halo // 8) * 8
    kernel = functools.partial(_mlp_kernel, tm=tm, shift=shift)
    tok = lambda bi, i: (bi, i, 0)
    const = lambda bi, i: (0, 0)
    return pl.pallas_call(
        kernel,
        out_shape=(jax.ShapeDtypeStruct((b, length, D_MODEL), F32),
                   jax.ShapeDtypeStruct((b, halo, 2 * D_FF), F32)),
        grid=(b, length // tm),
        in_specs=[pl.BlockSpec((1, tm, D_MODEL), tok),
                  pl.BlockSpec((1, tm, GROUP_W), tok),
                  pl.BlockSpec((1, tm, GROUP_W), tok),
                  pl.BlockSpec((1, 6, mod_blk, D_MODEL),
                               (lambda bi, i: (bi, 0, 0, 0)) if l_mod == 1 else (lambda bi, i: (bi, 0, i, 0))),
                  _resident((2 * GROUP_W, D_MODEL), const),
                  pl.BlockSpec((1, D_MODEL), const),
                  _resident((D_MODEL, 2 * D_FF), const),
                  pl.BlockSpec((CONV_W, 2 * D_FF), const),
                  pl.BlockSpec((1, 2 * D_FF), const),
                  _resident((D_FF, D_MODEL), const),
                  pl.BlockSpec((1, D_MODEL), const),
                  pl.BlockSpec((1, halo, 2 * D_FF), lambda bi, i: (bi, 0, 0))],
        out_specs=(pl.BlockSpec((1, tm, D_MODEL), tok),
                   pl.BlockSpec((1, halo, 2 * D_FF), lambda bi, i: (bi, 0, 0))),
        scratch_shapes=[pltpu.VMEM((4, pad + tm, FF_CHUNK), F32),
                        pltpu.VMEM((tm, D_MODEL), F32),
                        pltpu.VMEM((tm, D_MODEL), BF16)],
        compiler_params=_cparams(2),
        name="mlp",
    )(x, oa, orr, mod, w_out_b, g_ffn, w_up_b, w_conv, b_conv, w_down_b, g_final, conv_prev)


def _rotary_tables(pos):
    half = HEAD_W // 2
    inv_freq = ROPE_BASE ** (-jnp.arange(half, dtype=F32) / half)
    ang = pos.astype(F32)[:, None] * inv_freq[None, :]
    cos = jnp.cos(ang)
    sin = jnp.sin(ang)
    return jnp.concatenate([cos, cos], axis=-1), jnp.concatenate([-sin, sin], axis=-1)


def _masked_bucket(q_pos, k_pos, keys_major):
    rel = k_pos[None, :] - q_pos[:, None]
    visible = (k_pos[None, :] // CHUNK) <= (q_pos[:, None] // CHUNK)
    bucket = jnp.where(visible, _t5_bucket(rel), -1).astype(jnp.int32)
    return bucket.T if keys_major else bucket


def kernel(x_prompt, x_sample, cache_k, cache_v, state_ret, state_conv, c_prompt, c_sample,
           w_ada, b_ada, g_mix, w_in, lambda_q1, lambda_k1, lambda_q2, lambda_k2,
           g_sub_a, g_sub_r, w_out, g_ffn, w_up, w_conv, b_conv, w_down, rel_bias, g_final):
    depth = w_in.shape[0]
    assert depth == 1
    l = 0
    nb_p, s_p, _ = x_prompt.shape
    nb_s, s_new, _ = x_sample.shape
    past = cache_k.shape[2]
    n_tok_s = nb_s * s_new

    log_gamma = jnp.log(1.0 - 2.0 ** (-5.0 - jnp.arange(R_HEADS, dtype=F32)))
    lam_init = 0.8 - 0.6 * math.exp(-0.3 * l)
    lam = (jnp.exp(jnp.sum(lambda_q1[l].astype(F32) * lambda_k1[l].astype(F32)))
           - jnp.exp(jnp.sum(lambda_q2[l].astype(F32) * lambda_k2[l].astype(F32))) + lam_init).reshape(1)

    w_in_b = w_in[l].astype(BF16)
    w_out_b = w_out[l].astype(BF16)
    w_up_b = w_up[l].astype(BF16)
    w_down_b = w_down[l].astype(BF16)
    g_mix_l = g_mix[l].reshape(1, D_MODEL)
    g_ffn_l = g_ffn[l].reshape(1, D_MODEL)
    g_fin = g_final.reshape(1, D_MODEL)
    g_sa_row = g_sub_a[l].reshape(1, HEAD_W)
    g_sa_col = g_sub_a[l].reshape(HEAD_W, 1)
    g_sr = g_sub_r[l].reshape(1, HEAD_W)
    w_conv_l = w_conv[l]
    b_conv_l = b_conv[l].reshape(1, 2 * D_FF)

    mods = _adaln(jnp.concatenate([c_prompt, c_sample], axis=0), w_ada[l], b_ada[l])
    mods = mods.reshape(nb_p + nb_s, 6, D_MODEL)
    mod_p = mods[:nb_p].reshape(nb_p, 6, 1, D_MODEL)
    mod_s = jnp.transpose(mods[nb_p:], (1, 0, 2))
    mod_s_stream = jnp.repeat(mod_s, s_new, axis=1)[None]
    mod_s_time = jnp.tile(mod_s, (1, s_new, 1))[None]

    t_attn = 256
    cos_p, sin_p = _rotary_tables(jnp.arange(s_p, dtype=jnp.int32))
    (k32_p, v32_p, qa_p, ka_p, _, qr_p, kr_p, vr_p, gr_p, vt_p) = _proj(
        x_prompt, mod_p, g_mix_l, w_in_b, cos_p, sin_p, tm=512, with_vt=True)

    tile_pos = jnp.arange(t_attn, dtype=jnp.int32) + t_attn
    bucket_tiles = jnp.stack([_masked_bucket(tile_pos, tile_pos, True),
                              _masked_bucket(tile_pos, tile_pos - t_attn, True),
                              jnp.full((t_attn, t_attn), FAR_BUCKET, jnp.int32)])
    bias_tiles = _bias(rel_bias, jnp.tile(bucket_tiles, (1, 1, 2)), shift=True)
    oa_p = _attn_prompt(lam, qa_p, ka_p, vt_p, bias_tiles, g_sa_col, lam_init, t_attn, group=4)

    or_p, ret_p = _retention(log_gamma, qr_p, kr_p, vr_p, gr_p,
                             jnp.zeros((nb_p, R_HEADS, HEAD_W, HEAD_W), F32), g_sr, c=256)

    y_p, conv_p = _mlp(x_prompt, oa_p, or_p, mod_p, w_out_b, g_ffn_l, w_up_b, w_conv_l, b_conv_l,
                       w_down_b, g_fin, jnp.zeros((nb_p, CONV_W - 1, 2 * D_FF), F32), tm=512, shift=1)

    pos_s = past + jnp.arange(s_new, dtype=jnp.int32)
    cos_s, sin_s = _rotary_tables(pos_s)
    cos_s = jnp.tile(cos_s, (nb_s, 1))
    sin_s = jnp.tile(sin_s, (nb_s, 1))
    (k32_s, v32_s, qa_s, ka_s, va_s, qr_s, kr_s, vr_s, gr_s) = _proj(
        x_sample.reshape(1, n_tok_s, D_MODEL), mod_s_stream, g_mix_l, w_in_b, cos_s, sin_s, tm=n_tok_s)

    def streams(t):
        return t.reshape(nb_s, s_new, GROUP_W)

    bucket_s = _masked_bucket(pos_s, jnp.arange(past + s_new, dtype=jnp.int32), False)
    bias_c = _bias(rel_bias, bucket_s[None, :, :past], shift=False)
    bias_n = _bias(rel_bias, bucket_s[None, :, past:], shift=False)
    oa_s = _attn_sample(lam, streams(qa_s), cache_k[l].reshape(nb_s, past * A_HEADS, HEAD_W),
                        cache_v[l].reshape(nb_s, past * A_HEADS, HEAD_W), streams(ka_s), streams(va_s),
                        bias_c, bias_n, g_sa_row, lam_init)

    or_s, ret_s = _retention(log_gamma, streams(qr_s), streams(kr_s), streams(vr_s), streams(gr_s),
                             state_ret[l].astype(F32), g_sr, c=s_new)

    def time_major(t):
        return jnp.transpose(t, (1, 0, 2)).reshape(1, n_tok_s, t.shape[-1])

    conv_prev_s = jnp.transpose(state_conv[l], (1, 0, 2)).reshape(1, (CONV_W - 1) * nb_s, 2 * D_FF)
    y_s, conv_s = _mlp(time_major(x_sample), time_major(oa_s), time_major(or_s), mod_s_time,
                       w_out_b, g_ffn_l, w_up_b, w_conv_l, b_conv_l, w_down_b, g_fin, conv_prev_s,
                       tm=n_tok_s, shift=nb_s)
    y_s = jnp.transpose(y_s.reshape(s_new, nb_s, D_MODEL), (1, 0, 2))
    conv_s = jnp.transpose(conv_s.reshape(CONV_W - 1, nb_s, 2 * D_FF), (1, 0, 2))

    def heads(t, n):
        return t.reshape(1, n, -1, A_HEADS, HEAD_W)

    return (y_p, y_s, heads(k32_p, nb_p), heads(v32_p, nb_p), ret_p[None], conv_p[None],
            heads(k32_s, nb_s), heads(v32_s, nb_s), ret_s[None], conv_s[None])
```

```python
import functools
import math

import jax
import jax.numpy as jnp
from jax import lax
from jax.experimental import pallas as pl
from jax.experimental.pallas import tpu as pltpu

F32 = jnp.float32
BF16 = jnp.bfloat16

D_MODEL = 1024
CHUNK = 64
A_HEADS = 4
A_QK_DIM = 64
HEAD_W = 128
GROUP_W = A_HEADS * HEAD_W
N_GROUPS = 7
R_HEADS = 4
D_FF = 2816
CONV_W = 3
N_BUCKETS = 32
FAR_BUCKET = N_BUCKETS // 2 - 1
MAX_DISTANCE = 128
ROPE_BASE = 10000.0
EPS = 1e-6
NEG_INF = -1e30

VMEM_LIMIT = 56 * 1024 * 1024
FF_CHUNK = 256
LOG2E = math.log2(math.e)
Q_SCALE = A_QK_DIM ** -0.5 * LOG2E
VT_ROWS = HEAD_W + 16


def _cparams(n_axes, flags=None):
    return pltpu.CompilerParams(dimension_semantics=("arbitrary",) * n_axes,
                                vmem_limit_bytes=VMEM_LIMIT, flags=flags)


def _resident(block_shape, index_map):
    return pl.BlockSpec(block_shape, index_map, pipeline_mode=pl.Buffered(1))


def _rms(x):
    return x * lax.rsqrt(jnp.mean(x * x, axis=-1, keepdims=True) + EPS)


def _silu(x):
    return x * jax.nn.sigmoid(x)


def _dot_nt(a, b):
    return lax.dot_general(a, b, (((1,), (1,)), ((), ())), preferred_element_type=F32)


def _dot_tn(a, b):
    return lax.dot_general(a, b, (((0,), (0,)), ((), ())), preferred_element_type=F32)


def _adaln_kernel(c_ref, w_ref, b_ref, o_ref):
    a = _silu(c_ref[...]).astype(BF16)
    o_ref[...] = jnp.dot(a, w_ref[...].astype(BF16), preferred_element_type=F32) + b_ref[...]


def _adaln(c, w_ada, b_ada):
    n_rows = c.shape[0]
    n_cols = w_ada.shape[1]
    tn = 1536
    return pl.pallas_call(
        _adaln_kernel,
        out_shape=jax.ShapeDtypeStruct((n_rows, n_cols), F32),
        grid=(n_cols // tn,),
        in_specs=[pl.BlockSpec((n_rows, D_MODEL), lambda j: (0, 0)),
                  pl.BlockSpec((D_MODEL, tn), lambda j: (0, j)),
                  pl.BlockSpec((1, tn), lambda j: (0, j))],
        out_specs=pl.BlockSpec((n_rows, tn), lambda j: (0, j)),
        compiler_params=_cparams(1),
        name="adaln",
    )(c, w_ada, b_ada.reshape(1, n_cols))


def _proj_kernel(x_ref, mod_ref, g_ref, w_ref, cos_ref, sin_ref,
                 k32_ref, v32_ref, qa_ref, ka_ref, va_ref, qr_ref, kr_ref, vr_ref, gr_ref, *maybe_vt_ref,
                 tm):
    x = x_ref[0]
    h = _rms(x) * g_ref[...]
    h = h * (1.0 + mod_ref[0, 1]) + mod_ref[0, 0]
    hb = h.astype(BF16)

    def group(g):
        return jnp.dot(hb, w_ref[:, g * GROUP_W:(g + 1) * GROUP_W], preferred_element_type=F32)

    def store_heads(ref, t):
        for hh in range(A_HEADS):
            ref[0, pl.ds(hh, tm, stride=A_HEADS), :] = t[:, hh * HEAD_W:(hh + 1) * HEAD_W]

    qa_ref[0] = (group(0) * Q_SCALE).astype(BF16)
    ka = group(1)
    store_heads(k32_ref, ka)
    ka_ref[0] = ka.astype(BF16)
    va = group(2)
    store_heads(v32_ref, va)
    va_ref[0] = va.astype(BF16)
    if maybe_vt_ref:
        vt_ref = maybe_vt_ref[0]
        vt = va.T.astype(BF16)
        ones = jnp.ones((VT_ROWS - HEAD_W, tm), BF16)
        for hh in range(A_HEADS):
            vt_ref[0, hh * VT_ROWS:hh * VT_ROWS + HEAD_W, :] = vt[hh * HEAD_W:(hh + 1) * HEAD_W, :]
            vt_ref[0, hh * VT_ROWS + HEAD_W:(hh + 1) * VT_ROWS, :] = ones

    cos = cos_ref[...]
    sin = sin_ref[...]

    def rotary(t, out_ref, scale):
        for hh in range(R_HEADS):
            th = t[:, hh * HEAD_W:(hh + 1) * HEAD_W]
            r = th * cos + pltpu.roll(th, HEAD_W // 2, 1) * sin
            if scale != 1.0:
                r = r * scale
            out_ref[0, :, hh * HEAD_W:(hh + 1) * HEAD_W] = r.astype(BF16)

    rotary(group(3), qr_ref, 1.0)
    rotary(group(4), kr_ref, HEAD_W ** -0.5)
    vr_ref[0] = group(5).astype(BF16)
    gr_ref[0] = group(6).astype(BF16)


def _proj(x, mod, g_mix, w_in_b, cos_t, sin_t, tm, with_vt=False):
    b, length, _ = x.shape
    l_mod = mod.shape[2]
    mod_blk = 1 if l_mod == 1 else tm
    tok = lambda bi, i: (bi, i, 0)
    out_tok = pl.BlockSpec((1, tm, GROUP_W), tok)
    out_heads = pl.BlockSpec((1, tm * A_HEADS, HEAD_W), tok)
    heads_out = jax.ShapeDtypeStruct((b, length * A_HEADS, HEAD_W), F32)
    bf_out = jax.ShapeDtypeStruct((b, length, GROUP_W), BF16)
    out_shape = (heads_out, heads_out) + (bf_out,) * 7
    out_specs = (out_heads, out_heads) + (out_tok,) * 7
    if with_vt:
        out_shape += (jax.ShapeDtypeStruct((b, A_HEADS * VT_ROWS, length), BF16),)
        out_specs += (pl.BlockSpec((1, A_HEADS * VT_ROWS, tm), lambda bi, i: (bi, 0, i)),)
    return pl.pallas_call(
        functools.partial(_proj_kernel, tm=tm),
        out_shape=out_shape,
        grid=(b, length // tm),
        in_specs=[pl.BlockSpec((1, tm, D_MODEL), tok),
                  pl.BlockSpec((1, 6, mod_blk, D_MODEL),
                               (lambda bi, i: (bi, 0, 0, 0)) if l_mod == 1 else (lambda bi, i: (bi, 0, i, 0))),
                  pl.BlockSpec((1, D_MODEL), lambda bi, i: (0, 0)),
                  _resident((D_MODEL, N_GROUPS * GROUP_W), lambda bi, i: (0, 0)),
                  pl.BlockSpec((tm, HEAD_W), lambda bi, i: (i, 0)),
                  pl.BlockSpec((tm, HEAD_W), lambda bi, i: (i, 0))],
        out_specs=out_specs,
        compiler_params=_cparams(2),
        name="proj",
    )(x, mod, g_mix, w_in_b, cos_t, sin_t)


def _t5_bucket(rel):
    half = N_BUCKETS // 2
    max_exact = half // 2
    ret = jnp.where(rel > 0, half, 0)
    n = jnp.abs(rel)
    large = max_exact + (jnp.log(jnp.maximum(n, 1).astype(jnp.float32) / max_exact)
                         / math.log(MAX_DISTANCE / max_exact) * (half - max_exact)).astype(jnp.int32)
    large = jnp.minimum(large, half - 1)
    return ret + jnp.where(n < max_exact, n, large)


def _bias_kernel(rb_ref, bucket_ref, o_ref, *, row_blk, shift):
    def rows(r, carry):
        sl = pl.ds(pl.multiple_of(r * row_blk, row_blk), row_blk)
        b = bucket_ref[0, sl, :]
        acc = [jnp.zeros(b.shape, F32) for _ in range(A_HEADS)]
        for n in range(N_BUCKETS):
            hit = b == n
            for hh in range(A_HEADS):
                acc[hh] = jnp.where(hit, rb_ref[n * A_HEADS + hh], acc[hh])
        for hh in range(A_HEADS):
            val = acc[hh] - rb_ref[FAR_BUCKET * A_HEADS + hh] if shift else acc[hh]
            o_ref[hh, 0, sl, :] = jnp.where(b < 0, NEG_INF, val * LOG2E)
        return carry

    lax.fori_loop(0, bucket_ref.shape[1] // row_blk, rows, 0)


def _bias(rel_bias, bucket, shift):
    n, r, c = bucket.shape
    row_blk = 16 if c <= 1024 else 8
    return pl.pallas_call(
        functools.partial(_bias_kernel, row_blk=row_blk, shift=shift),
        out_shape=jax.ShapeDtypeStruct((A_HEADS, n, r, c), F32),
        grid=(n,),
        in_specs=[pl.BlockSpec(memory_space=pltpu.SMEM),
                  pl.BlockSpec((1, r, c), lambda j: (j, 0, 0))],
        out_specs=pl.BlockSpec((A_HEADS, 1, r, c), lambda j: (0, j, 0, 0)),
        compiler_params=_cparams(1),
        name="bias",
    )(rel_bias.astype(F32).reshape(N_BUCKETS * A_HEADS), bucket)


def _stack_query(q, rows):
    lane = lax.broadcasted_iota(jnp.int32, (rows, HEAD_W), 1)
    zero = jnp.zeros_like(q)
    return jnp.concatenate([jnp.where(lane < A_QK_DIM, q, zero),
                            jnp.where(lane >= A_QK_DIM, q, zero)], axis=0)


def _attn_prompt_kernel(lam_ref, q_ref, k_ref, vt_ref, bias_ref, g_ref, o_ref,
                        qs_scr, s1_scr, sg_scr, m_scr, acc_scr, *, t, group, nh, lam_init):
    i = pl.program_id(2)
    tg = group * t
    heads = range(nh)
    for hh in heads:
        qs_scr[hh] = _stack_query(q_ref[0, :, hh * HEAD_W:(hh + 1) * HEAD_W], t)
    acc_scr[...] = jnp.zeros_like(acc_scr)
    m_scr[...] = jnp.full_like(m_scr, -jnp.inf)

    def scores(k0, n_keys):
        rows = pl.ds(pl.multiple_of(k0, t), n_keys)
        return [_dot_nt(k_ref[0, rows, hh * HEAD_W:(hh + 1) * HEAD_W], qs_scr[hh]) for hh in heads]

    def keep(s_scr, s, bias_sel=None):
        out = []
        for hh in heads:
            sh = s[hh] if bias_sel is None else s[hh] + bias_ref[hh, bias_sel]
            s_scr[hh] = sh
            out.append(jnp.max(sh, axis=0, keepdims=True))
        return tuple(out)

    def fold(s_scr, k0, n_keys, s_max):
        cols = pl.ds(pl.multiple_of(k0, t), n_keys)
        for hh in heads:
            m_prev = m_scr[hh]
            m_next = jnp.maximum(m_prev, s_max[hh])
            alpha = jnp.exp2(m_prev - m_next)
            p = jnp.exp2(s_scr[hh] - m_next).astype(BF16)
            vt = vt_ref[0, hh * VT_ROWS:(hh + 1) * VT_ROWS, cols]
            acc_scr[hh] = alpha * acc_scr[hh] + jnp.dot(vt, p, preferred_element_type=F32)
            m_scr[hh] = m_next

    n_near = jnp.where(i >= 1, 2 + lax.rem(i - 1, group), 1)
    n_groups = (i + 1 - n_near) // group

    def near_body(r, s_max):
        s_next = scores((i - r) * t, t)
        fold(s1_scr, (i - r + 1) * t, t, s_max)
        return keep(s1_scr, s_next, jnp.minimum(r, 2))

    s_max = keep(s1_scr, scores(i * t, t), 0)
    s_max = lax.fori_loop(1, n_near, near_body, s_max)
    s_far = scores(0, tg)
    fold(s1_scr, (i - n_near + 1) * t, t, s_max)
    s_max = keep(sg_scr, s_far)

    def far_body(g, s_max):
        s_next = scores(g * tg, tg)
        fold(sg_scr, (g - 1) * tg, tg, s_max)
        return keep(sg_scr, s_next)

    s_max = lax.fori_loop(1, n_groups, far_body, s_max)

    @pl.when(n_groups > 0)
    def _():
        fold(sg_scr, (n_groups - 1) * tg, tg, s_max)

    for hh in heads:
        acc = acc_scr[hh]
        o_all = acc[:HEAD_W] / acc[HEAD_W:HEAD_W + 1]
        o = o_all[:, :t] - lam_ref[0] * o_all[:, t:]
        o = o * lax.rsqrt(jnp.mean(o * o, axis=0, keepdims=True) + EPS) * (g_ref[...] * (1.0 - lam_init))
        o_ref[0, :, hh * HEAD_W:(hh + 1) * HEAD_W] = o.T.astype(BF16)


def _attn_prompt(lam, qa, ka, vt, bias_tiles, g_col, lam_init, t, group, nh):
    b, s, _ = qa.shape
    kernel = functools.partial(_attn_prompt_kernel, t=t, group=group, nh=nh, lam_init=lam_init)
    tok = pl.BlockSpec((1, t, nh * HEAD_W), lambda bi, h, i: (bi, i, h))
    return pl.pallas_call(
        kernel,
        out_shape=jax.ShapeDtypeStruct((b, s, GROUP_W), BF16),
        grid=(b, A_HEADS // nh, s // t),
        in_specs=[pl.BlockSpec(memory_space=pltpu.SMEM),
                  tok,
                  pl.BlockSpec((1, s, nh * HEAD_W), lambda bi, h, i: (bi, 0, h)),
                  pl.BlockSpec((1, nh * VT_ROWS, s), lambda bi, h, i: (bi, h, 0)),
                  pl.BlockSpec((nh, 3, t, 2 * t), lambda bi, h, i: (h, 0, 0, 0)),
                  pl.BlockSpec((HEAD_W, 1), lambda bi, h, i: (0, 0))],
        out_specs=tok,
        scratch_shapes=[pltpu.VMEM((nh, 2 * t, HEAD_W), BF16),
                        pltpu.VMEM((nh, t, 2 * t), F32),
                        pltpu.VMEM((nh, group * t, 2 * t), F32),
                        pltpu.VMEM((nh, 1, 2 * t), F32),
                        pltpu.VMEM((nh, VT_ROWS, 2 * t), F32)],
        compiler_params=_cparams(3),
        name="attn_prompt",
    )(lam, qa, ka, vt, bias_tiles, g_col)


def _attn_sample_kernel(lam_ref, q_ref, kc_ref, vc_ref, kn_ref, vn_ref, bc_ref, bn_ref, g_ref, o_ref,
                        *, rows, past, lam_init):
    lam = lam_ref[0]
    for hh in range(A_HEADS):
        sl = slice(hh * HEAD_W, (hh + 1) * HEAD_W)
        qs = _stack_query(q_ref[0, :, sl], rows)
        kc = kc_ref[0, pl.ds(hh, past, stride=A_HEADS), :].astype(BF16)
        vc = vc_ref[0, pl.ds(hh, past, stride=A_HEADS), :].astype(BF16)
        bc = bc_ref[hh, 0]
        bn = bn_ref[hh, 0]
        s_c = _dot_nt(qs, kc) + jnp.concatenate([bc, bc], axis=0)
        s_n = _dot_nt(qs, kn_ref[0, :, sl]) + jnp.concatenate([bn, bn], axis=0)
        m = jnp.maximum(jnp.max(s_c, axis=1, keepdims=True), jnp.max(s_n, axis=1, keepdims=True))
        p_c = jnp.exp2(s_c - m)
        p_n = jnp.exp2(s_n - m)
        inv_l = 1.0 / (jnp.sum(p_c, axis=1, keepdims=True) + jnp.sum(p_n, axis=1, keepdims=True))
        p_c = p_c * inv_l
        p_n = p_n * inv_l
        d_c = (p_c[:rows] - lam * p_c[rows:]).astype(BF16)
        d_n = (p_n[:rows] - lam * p_n[rows:]).astype(BF16)
        o = (jnp.dot(d_c, vc, preferred_element_type=F32)
             + jnp.dot(d_n, vn_ref[0, :, sl], preferred_element_type=F32))
        o_ref[0, :, sl] = (_rms(o) * g_ref[...] * (1.0 - lam_init)).astype(BF16)


def _attn_sample(lam, qa, cache_k, cache_v, ka, va, bias_c, bias_n, g_row, lam_init):
    nb, rows, _ = qa.shape
    past = cache_k.shape[1] // A_HEADS
    kernel = functools.partial(_attn_sample_kernel, rows=rows, past=past, lam_init=lam_init)
    new_spec = pl.BlockSpec((1, rows, GROUP_W), lambda bi: (bi, 0, 0))
    cache_spec = pl.BlockSpec((1, past * A_HEADS, HEAD_W), lambda bi: (bi, 0, 0))
    return pl.pallas_call(
        kernel,
        out_shape=jax.ShapeDtypeStruct((nb, rows, GROUP_W), BF16),
        grid=(nb,),
        in_specs=[pl.BlockSpec(memory_space=pltpu.SMEM),
                  new_spec, cache_spec, cache_spec, new_spec, new_spec,
                  pl.BlockSpec((A_HEADS, 1, rows, past), lambda bi: (0, 0, 0, 0)),
                  pl.BlockSpec((A_HEADS, 1, rows, rows), lambda bi: (0, 0, 0, 0)),
                  pl.BlockSpec((1, HEAD_W), lambda bi: (0, 0))],
        out_specs=new_spec,
        compiler_params=_cparams(1),
        name="attn_sample",
    )(lam, qa, cache_k, cache_v, ka, va, bias_c, bias_n, g_row)


def _retention_kernel(lg_ref, q_ref, k_ref, v_ref, g_ref, s0_ref, gsub_ref, o_ref, st_ref, *, c):
    @pl.when(pl.program_id(1) == 0)
    def _():
        st_ref[...] = s0_ref[...]

    row = lax.broadcasted_iota(jnp.int32, (c, c), 0)
    col = lax.broadcasted_iota(jnp.int32, (c, c), 1)
    diff = (row - col).astype(F32)
    causal = diff >= 0.0
    diff = jnp.maximum(diff, 0.0)
    idx = lax.broadcasted_iota(jnp.int32, (c, 1), 0).astype(F32)
    gsub = gsub_ref[...]

    for hh in range(R_HEADS):
        lg = lg_ref[hh]
        sl = slice(hh * HEAD_W, (hh + 1) * HEAD_W)
        q = q_ref[0, :, sl]
        k = k_ref[0, :, sl]
        v = v_ref[0, :, sl]
        decay = jnp.where(causal, jnp.exp(lg * diff), 0.0)
        scores = _dot_nt(q, k) * decay
        state = st_ref[0, hh]
        o = jnp.dot(scores.astype(BF16), v, preferred_element_type=F32)
        o = o + jnp.dot(q, state.astype(BF16), preferred_element_type=F32) * jnp.exp(lg * (idx + 1.0))
        kd = (k.astype(F32) * jnp.exp(lg * (c - 1.0 - idx))).astype(BF16)
        st_ref[0, hh] = jnp.exp(lg * c) * state + _dot_tn(kd, v)
        gate = g_ref[0, :, sl].astype(F32)
        o_ref[0, :, sl] = (_rms(o) * gsub * _silu(gate)).astype(BF16)


def _retention(log_gamma, qr, kr, vr, gr, state0, g_sub_r, c):
    b, s, _ = qr.shape
    kernel = functools.partial(_retention_kernel, c=c)
    tok = pl.BlockSpec((1, c, GROUP_W), lambda bi, i: (bi, i, 0))
    st = pl.BlockSpec((1, R_HEADS, HEAD_W, HEAD_W), lambda bi, i: (bi, 0, 0, 0))
    return pl.pallas_call(
        kernel,
        out_shape=(jax.ShapeDtypeStruct((b, s, GROUP_W), BF16),
                   jax.ShapeDtypeStruct((b, R_HEADS, HEAD_W, HEAD_W), F32)),
        grid=(b, s // c),
        in_specs=[pl.BlockSpec(memory_space=pltpu.SMEM), tok, tok, tok, tok, st,
                  pl.BlockSpec((1, HEAD_W), lambda bi, i: (0, 0))],
        out_specs=(tok, st),
        compiler_params=_cparams(2),
        name="retention",
    )(log_gamma, qr, kr, vr, gr, state0, g_sub_r)


def _mlp_kernel(x_ref, oa_ref, or_ref, mod_ref, wout_ref, gffn_ref, wup_ref, wconv_ref, bconv_ref,
                wdown_ref, gfin_ref, cprev_ref, y_ref, cst_ref, ue_scr, acc_scr, hb_scr, *, tm, shift):
    halo = (CONV_W - 1) * shift
    pad = -(-halo // 8) * 8

    @pl.when(pl.program_id(1) == 0)
    def _():
        cst_ref[...] = cprev_ref[...]

    mixed = (jnp.dot(oa_ref[0], wout_ref[:GROUP_W, :], preferred_element_type=F32)
             + jnp.dot(or_ref[0], wout_ref[GROUP_W:, :], preferred_element_type=F32))
    x1 = x_ref[0] + mod_ref[0, 2] * mixed
    y_ref[0] = x1
    h = _rms(x1) * gffn_ref[...]
    h = h * (1.0 + mod_ref[0, 4]) + mod_ref[0, 3]
    hb_scr[...] = h.astype(BF16)

    def up_half(slot, col0):
        cols = slice(col0, col0 + FF_CHUNK)
        ue_scr[slot, pad - halo:pad, :] = cst_ref[0, :, cols]
        ue_scr[slot, pad:pad + tm, :] = jnp.dot(hb_scr[...], wup_ref[:, cols], preferred_element_type=F32)
        cst_ref[0, :, cols] = ue_scr[slot, pad + tm - halo:pad + tm, :]

    def up(ch):
        up_half(2 * (ch % 2), ch * FF_CHUNK)
        up_half(2 * (ch % 2) + 1, D_FF + ch * FF_CHUNK)

    def conv_half(slot, col0):
        cols = slice(col0, col0 + FF_CHUNK)
        y = bconv_ref[:, cols]
        for j in range(CONV_W):
            start = pad - halo + j * shift
            y = y + wconv_ref[j:j + 1, cols] * ue_scr[slot, start:start + tm, :]
        return y

    n_chunks = D_FF // FF_CHUNK
    up(0)
    for ch in range(n_chunks):
        if ch + 1 < n_chunks:
            up(ch + 1)
        ya = conv_half(2 * (ch % 2), ch * FF_CHUNK)
        yg = conv_half(2 * (ch % 2) + 1, D_FF + ch * FF_CHUNK)
        act = (_silu(ya) * yg).astype(BF16)
        part = jnp.dot(act, wdown_ref[ch * FF_CHUNK:(ch + 1) * FF_CHUNK, :], preferred_element_type=F32)
        if ch == 0:
            acc_scr[...] = part
        else:
            acc_scr[...] += part

    x2 = y_ref[0] + mod_ref[0, 5] * acc_scr[...]
    y_ref[0] = _rms(x2) * gfin_ref[...]


def _mlp(x, oa, orr, mod, w_out_b, g_ffn, w_up_b, w_conv, b_conv, w_down_b, g_final, conv_prev, tm, shift):
    b, length, _ = x.shape
    l_mod = mod.shape[2]
    mod_blk = 1 if l_mod == 1 else tm
    halo = (CONV_W - 1) * shift
    pad = -(-halo // 8) * 8
    kernel = functools.partial(_mlp_kernel, tm=tm, shift=shift)
    tok = lambda bi, i: (bi, i, 0)
    const = lambda bi, i: (0, 0)
    return pl.pallas_call(
        kernel,
        out_shape=(jax.ShapeDtypeStruct((b, length, D_MODEL), F32),
                   jax.ShapeDtypeStruct((b, halo, 2 * D_FF), F32)),
        grid=(b, length // tm),
        in_specs=[pl.BlockSpec((1, tm, D_MODEL), tok),
                  pl.BlockSpec((1, tm, GROUP_W), tok),
                  pl.BlockSpec((1, tm, GROUP_W), tok),
                  pl.BlockSpec((1, 6, mod_blk, D_MODEL),
                               (lambda bi, i: (bi, 0, 0, 0)) if l_mod == 1 else (lambda bi, i: (bi, 0, i, 0))),
                  _resident((2 * GROUP_W, D_MODEL), const),
                  pl.BlockSpec((1, D_MODEL), const),
                  _resident((D_MODEL, 2 * D_FF), const),
                  pl.BlockSpec((CONV_W, 2 * D_FF), const),
                  pl.BlockSpec((1, 2 * D_FF), const),
                  _resident((D_FF, D_MODEL), const),
                  pl.BlockSpec((1, D_MODEL), const),
                  pl.BlockSpec((1, halo, 2 * D_FF), lambda bi, i: (bi, 0, 0))],
        out_specs=(pl.BlockSpec((1, tm, D_MODEL), tok),
                   pl.BlockSpec((1, halo, 2 * D_FF), lambda bi, i: (bi, 0, 0))),
        scratch_shapes=[pltpu.VMEM((4, pad + tm, FF_CHUNK), F32),
                        pltpu.VMEM((tm, D_MODEL), F32),
                        pltpu.VMEM((tm, D_MODEL), BF16)],
        compiler_params=_cparams(2),
        name="mlp",
    )(x, oa, orr, mod, w_out_b, g_ffn, w_up_b, w_conv, b_conv, w_down_b, g_final, conv_prev)


def _rotary_tables(pos):
    half = HEAD_W // 2
    inv_freq = ROPE_BASE ** (-jnp.arange(half, dtype=F32) / half)
    ang = pos.astype(F32)[:, None] * inv_freq[None, :]
    cos = jnp.cos(ang)
    sin = jnp.sin(ang)
    return jnp.concatenate([cos, cos], axis=-1), jnp.concatenate([-sin, sin], axis=-1)


def _masked_bucket(q_pos, k_pos, keys_major):
    rel = k_pos[None, :] - q_pos[:, None]
    visible = (k_pos[None, :] // CHUNK) <= (q_pos[:, None] // CHUNK)
    bucket = jnp.where(visible, _t5_bucket(rel), -1).astype(jnp.int32)
    return bucket.T if keys_major else bucket


def kernel(x_prompt, x_sample, cache_k, cache_v, state_ret, state_conv, c_prompt, c_sample,
           w_ada, b_ada, g_mix, w_in, lambda_q1, lambda_k1, lambda_q2, lambda_k2,
           g_sub_a, g_sub_r, w_out, g_ffn, w_up, w_conv, b_conv, w_down, rel_bias, g_final):
    depth = w_in.shape[0]
    assert depth == 1
    l = 0
    nb_p, s_p, _ = x_prompt.shape
    nb_s, s_new, _ = x_sample.shape
    past = cache_k.shape[2]
    n_tok_s = nb_s * s_new

    log_gamma = jnp.log(1.0 - 2.0 ** (-5.0 - jnp.arange(R_HEADS, dtype=F32)))
    lam_init = 0.8 - 0.6 * math.exp(-0.3 * l)
    lam = (jnp.exp(jnp.sum(lambda_q1[l].astype(F32) * lambda_k1[l].astype(F32)))
           - jnp.exp(jnp.sum(lambda_q2[l].astype(F32) * lambda_k2[l].astype(F32))) + lam_init).reshape(1)

    w_in_b = w_in[l].astype(BF16)
    w_out_b = w_out[l].astype(BF16)
    w_up_b = w_up[l].astype(BF16)
    w_down_b = w_down[l].astype(BF16)
    g_mix_l = g_mix[l].reshape(1, D_MODEL)
    g_ffn_l = g_ffn[l].reshape(1, D_MODEL)
    g_fin = g_final.reshape(1, D_MODEL)
    g_sa_row = g_sub_a[l].reshape(1, HEAD_W)
    g_sa_col = g_sub_a[l].reshape(HEAD_W, 1)
    g_sr = g_sub_r[l].reshape(1, HEAD_W)
    w_conv_l = w_conv[l]
    b_conv_l = b_conv[l].reshape(1, 2 * D_FF)

    mods = _adaln(jnp.concatenate([c_prompt, c_sample], axis=0), w_ada[l], b_ada[l])
    mods = mods.reshape(nb_p + nb_s, 6, D_MODEL)
    mod_p = mods[:nb_p].reshape(nb_p, 6, 1, D_MODEL)
    mod_s = jnp.transpose(mods[nb_p:], (1, 0, 2))
    mod_s_stream = jnp.repeat(mod_s, s_new, axis=1)[None]
    mod_s_time = jnp.tile(mod_s, (1, s_new, 1))[None]

    t_attn = 256
    cos_p, sin_p = _rotary_tables(jnp.arange(s_p, dtype=jnp.int32))
    (k32_p, v32_p, qa_p, ka_p, _, qr_p, kr_p, vr_p, gr_p, vt_p) = _proj(
        x_prompt, mod_p, g_mix_l, w_in_b, cos_p, sin_p, tm=512, with_vt=True)

    tile_pos = jnp.arange(t_attn, dtype=jnp.int32) + t_attn
    bucket_tiles = jnp.stack([_masked_bucket(tile_pos, tile_pos, True),
                              _masked_bucket(tile_pos, tile_pos - t_attn, True),
                              jnp.full((t_attn, t_attn), FAR_BUCKET, jnp.int32)])
    bias_tiles = _bias(rel_bias, jnp.tile(bucket_tiles, (1, 1, 2)), shift=True)
    oa_p = _attn_prompt(lam, qa_p, ka_p, vt_p, bias_tiles, g_sa_col, lam_init, t_attn, group=4, nh=2)

    or_p, ret_p = _retention(log_gamma, qr_p, kr_p, vr_p, gr_p,
                             jnp.zeros((nb_p, R_HEADS, HEAD_W, HEAD_W), F32), g_sr, c=256)

    y_p, conv_p = _mlp(x_prompt, oa_p, or_p, mod_p, w_out_b, g_ffn_l, w_up_b, w_conv_l, b_conv_l,
                       w_down_b, g_fin, jnp.zeros((nb_p, CONV_W - 1, 2 * D_FF), F32), tm=512, shift=1)

    pos_s = past + jnp.arange(s_new, dtype=jnp.int32)
    cos_s, sin_s = _rotary_tables(pos_s)
    cos_s = jnp.tile(cos_s, (nb_s, 1))
    sin_s = jnp.tile(sin_s, (nb_s, 1))
    (k32_s, v32_s, qa_s, ka_s, va_s, qr_s, kr_s, vr_s, gr_s) = _proj(
        x_sample.reshape(1, n_tok_s, D_MODEL), mod_s_stream, g_mix_l, w_in_b, cos_s, sin_s, tm=n_tok_s)

    def streams(t):
        return t.reshape(nb_s, s_new, GROUP_W)

    bucket_s = _masked_bucket(pos_s, jnp.arange(past + s_new, dtype=jnp.int32), False)
    bias_c = _bias(rel_bias, bucket_s[None, :, :past], shift=False)
    bias_n = _bias(rel_bias, bucket_s[None, :, past:], shift=False)
    oa_s = _attn_sample(lam, streams(qa_s), cache_k[l].reshape(nb_s, past * A_HEADS, HEAD_W),
                        cache_v[l].reshape(nb_s, past * A_HEADS, HEAD_W), streams(ka_s), streams(va_s),
                        bias_c, bias_n, g_sa_row, lam_init)

    or_s, ret_s = _retention(log_gamma, streams(qr_s), streams(kr_s), streams(vr_s), streams(gr_s),
                             state_ret[l].astype(F32), g_sr, c=s_new)

    def time_major(t):
        return jnp.transpose(t, (1, 0, 2)).reshape(1, n_tok_s, t.shape[-1])

    conv_prev_s = jnp.transpose(state_conv[l], (1, 0, 2)).reshape(1, (CONV_W - 1) * nb_s, 2 * D_FF)
    y_s, conv_s = _mlp(time_major(x_sample), time_major(oa_s), time_major(or_s), mod_s_time,
                       w_out_b, g_ffn_l, w_up_b, w_conv_l, b_conv_l, w_down_b, g_fin, conv_prev_s,
                       tm=n_tok_s, shift=nb_s)
    y_s = jnp.transpose(y_s.reshape(s_new, nb_s, D_MODEL), (1, 0, 2))
    conv_s = jnp.transpose(conv_s.reshape(CONV_W - 1, nb_s, 2 * D_FF), (1, 0, 2))

    def heads(t, n):
        return t.reshape(1, n, -1, A_HEADS, HEAD_W)

    return (y_p, y_s, heads(k32_p, nb_p), heads(v32_p, nb_p), ret_p[None], conv_p[None],
            heads(k32_s, nb_s), heads(v32_s, nb_s), ret_s[None], conv_s[None])
```

```python
import functools
import math

import jax
import jax.numpy as jnp
from jax import lax
from jax.experimental import pallas as pl
from jax.experimental.pallas import tpu as pltpu

F32 = jnp.float32
BF16 = jnp.bfloat16

D_MODEL = 1024
CHUNK = 64
A_HEADS = 4
A_QK_DIM = 64
HEAD_W = 128
GROUP_W = A_HEADS * HEAD_W
N_GROUPS = 7
R_HEADS = 4
D_FF = 2816
CONV_W = 3
N_BUCKETS = 32
FAR_BUCKET = N_BUCKETS // 2 - 1
MAX_DISTANCE = 128
ROPE_BASE = 10000.0
EPS = 1e-6
NEG_INF = -1e30

VMEM_LIMIT = 56 * 1024 * 1024
FF_CHUNK = 256
LOG2E = math.log2(math.e)
Q_SCALE = A_QK_DIM ** -0.5 * LOG2E
VT_ROWS = HEAD_W + 16


def _cparams(n_axes, flags=None):
    return pltpu.CompilerParams(dimension_semantics=("arbitrary",) * n_axes,
                                vmem_limit_bytes=VMEM_LIMIT, flags=flags)


def _resident(block_shape, index_map):
    return pl.BlockSpec(block_shape, index_map, pipeline_mode=pl.Buffered(1))


def _rms(x):
    return x * lax.rsqrt(jnp.mean(x * x, axis=-1, keepdims=True) + EPS)


def _silu(x):
    return x * jax.nn.sigmoid(x)


def _dot_nt(a, b):
    return lax.dot_general(a, b, (((1,), (1,)), ((), ())), preferred_element_type=F32)


def _dot_tn(a, b):
    return lax.dot_general(a, b, (((0,), (0,)), ((), ())), preferred_element_type=F32)


def _adaln_kernel(c_ref, w_ref, b_ref, o_ref):
    a = _silu(c_ref[...]).astype(BF16)
    o_ref[...] = jnp.dot(a, w_ref[...].astype(BF16), preferred_element_type=F32) + b_ref[...]


def _adaln(c, w_ada, b_ada):
    n_rows = c.shape[0]
    n_cols = w_ada.shape[1]
    tn = 1536
    return pl.pallas_call(
        _adaln_kernel,
        out_shape=jax.ShapeDtypeStruct((n_rows, n_cols), F32),
        grid=(n_cols // tn,),
        in_specs=[pl.BlockSpec((n_rows, D_MODEL), lambda j: (0, 0)),
                  pl.BlockSpec((D_MODEL, tn), lambda j: (0, j)),
                  pl.BlockSpec((1, tn), lambda j: (0, j))],
        out_specs=pl.BlockSpec((n_rows, tn), lambda j: (0, j)),
        compiler_params=_cparams(1),
        name="adaln",
    )(c, w_ada, b_ada.reshape(1, n_cols))


def _proj_kernel(x_ref, mod_ref, g_ref, w_ref, cos_ref, sin_ref,
                 k32_ref, v32_ref, qa_ref, ka_ref, va_ref, qr_ref, kr_ref, vr_ref, gr_ref, *maybe_vt_ref,
                 tm):
    x = x_ref[0]
    h = _rms(x) * g_ref[...]
    h = h * (1.0 + mod_ref[0, 1]) + mod_ref[0, 0]
    hb = h.astype(BF16)

    def group(g):
        return jnp.dot(hb, w_ref[:, g * GROUP_W:(g + 1) * GROUP_W], preferred_element_type=F32)

    def store_heads(ref, t):
        for hh in range(A_HEADS):
            ref[0, pl.ds(hh, tm, stride=A_HEADS), :] = t[:, hh * HEAD_W:(hh + 1) * HEAD_W]

    qa_ref[0] = (group(0) * Q_SCALE).astype(BF16)
    ka = group(1)
    store_heads(k32_ref, ka)
    ka_ref[0] = ka.astype(BF16)
    va = group(2)
    store_heads(v32_ref, va)
    va_ref[0] = va.astype(BF16)
    if maybe_vt_ref:
        vt_ref = maybe_vt_ref[0]
        vt = va.T.astype(BF16)
        ones = jnp.ones((VT_ROWS - HEAD_W, tm), BF16)
        for hh in range(A_HEADS):
            vt_ref[0, hh * VT_ROWS:hh * VT_ROWS + HEAD_W, :] = vt[hh * HEAD_W:(hh + 1) * HEAD_W, :]
            vt_ref[0, hh * VT_ROWS + HEAD_W:(hh + 1) * VT_ROWS, :] = ones

    cos = cos_ref[...]
    sin = sin_ref[...]

    def rotary(t, out_ref, scale):
        for hh in range(R_HEADS):
            th = t[:, hh * HEAD_W:(hh + 1) * HEAD_W]
            r = th * cos + pltpu.roll(th, HEAD_W // 2, 1) * sin
            if scale != 1.0:
                r = r * scale
            out_ref[0, :, hh * HEAD_W:(hh + 1) * HEAD_W] = r.astype(BF16)

    rotary(group(3), qr_ref, 1.0)
    rotary(group(4), kr_ref, HEAD_W ** -0.5)
    vr_ref[0] = group(5).astype(BF16)
    gr_ref[0] = group(6).astype(BF16)


def _proj(x, mod, g_mix, w_in_b, cos_t, sin_t, tm, with_vt=False):
    b, length, _ = x.shape
    l_mod = mod.shape[2]
    mod_blk = 1 if l_mod == 1 else tm
    tok = lambda bi, i: (bi, i, 0)
    out_tok = pl.BlockSpec((1, tm, GROUP_W), tok)
    out_heads = pl.BlockSpec((1, tm * A_HEADS, HEAD_W), tok)
    heads_out = jax.ShapeDtypeStruct((b, length * A_HEADS, HEAD_W), F32)
    bf_out = jax.ShapeDtypeStruct((b, length, GROUP_W), BF16)
    out_shape = (heads_out, heads_out) + (bf_out,) * 7
    out_specs = (out_heads, out_heads) + (out_tok,) * 7
    if with_vt:
        out_shape += (jax.ShapeDtypeStruct((b, A_HEADS * VT_ROWS, length), BF16),)
        out_specs += (pl.BlockSpec((1, A_HEADS * VT_ROWS, tm), lambda bi, i: (bi, 0, i)),)
    return pl.pallas_call(
        functools.partial(_proj_kernel, tm=tm),
        out_shape=out_shape,
        grid=(b, length // tm),
        in_specs=[pl.BlockSpec((1, tm, D_MODEL), tok),
                  pl.BlockSpec((1, 6, mod_blk, D_MODEL),
                               (lambda bi, i: (bi, 0, 0, 0)) if l_mod == 1 else (lambda bi, i: (bi, 0, i, 0))),
                  pl.BlockSpec((1, D_MODEL), lambda bi, i: (0, 0)),
                  _resident((D_MODEL, N_GROUPS * GROUP_W), lambda bi, i: (0, 0)),
                  pl.BlockSpec((tm, HEAD_W), lambda bi, i: (i, 0)),
                  pl.BlockSpec((tm, HEAD_W), lambda bi, i: (i, 0))],
        out_specs=out_specs,
        compiler_params=_cparams(2),
        name="proj",
    )(x, mod, g_mix, w_in_b, cos_t, sin_t)


def _t5_bucket(rel):
    half = N_BUCKETS // 2
    max_exact = half // 2
    ret = jnp.where(rel > 0, half, 0)
    n = jnp.abs(rel)
    large = max_exact + (jnp.log(jnp.maximum(n, 1).astype(jnp.float32) / max_exact)
                         / math.log(MAX_DISTANCE / max_exact) * (half - max_exact)).astype(jnp.int32)
    large = jnp.minimum(large, half - 1)
    return ret + jnp.where(n < max_exact, n, large)


def _bias_vector(rel_bias, rel):
    return jnp.transpose(rel_bias[_t5_bucket(rel)].astype(F32))


def _bias_prompt_kernel(w_ref, o_ref, *, t):
    far = w_ref[0, 1:2, 2 * t - 1:2 * t]
    k_pos = lax.broadcasted_iota(jnp.int32, (t, t), 0)
    q_pos = lax.broadcasted_iota(jnp.int32, (t, t), 1)
    visible = (k_pos // CHUNK) <= (q_pos // CHUNK)
    for a in range(2):
        x = jnp.broadcast_to(w_ref[0, a:a + 1, :], (t, 2 * t))
        r = pltpu.roll(x, t, 1, stride=1, stride_axis=0)
        val = (r[:, :t] - far) * LOG2E
        if a == 0:
            val = jnp.where(visible, val, NEG_INF)
        o_ref[0, a] = jnp.concatenate([val, val], axis=1)
    o_ref[0, 2] = jnp.zeros((t, 2 * t), F32)


def _bias_prompt(w, t):
    return pl.pallas_call(
        functools.partial(_bias_prompt_kernel, t=t),
        out_shape=jax.ShapeDtypeStruct((A_HEADS, 3, t, 2 * t), F32),
        grid=(A_HEADS,),
        in_specs=[pl.BlockSpec((1, 2, 2 * t), lambda h: (h, 0, 0))],
        out_specs=pl.BlockSpec((1, 3, t, 2 * t), lambda h: (h, 0, 0, 0)),
        compiler_params=_cparams(1),
        name="bias_prompt",
    )(w)


def _bias_sample_kernel(w_ref, o_ref, *, rows, win, past, k0):
    x = jnp.broadcast_to(w_ref[0], (rows, win))
    r = pltpu.roll(x, 0, 1, stride=1, stride_axis=0)
    row = lax.broadcasted_iota(jnp.int32, (rows, win), 0)
    col = lax.broadcasted_iota(jnp.int32, (rows, win), 1)
    val = jnp.where(col >= row, r, w_ref[0, :, 0:1]) * LOG2E
    visible = ((k0 + col) // CHUNK) <= ((past + row) // CHUNK)
    o_ref[0] = jnp.where(visible, val, NEG_INF)


def _bias_sample(w, rows, past, k0):
    win = w.shape[-1]
    return pl.pallas_call(
        functools.partial(_bias_sample_kernel, rows=rows, win=win, past=past, k0=k0),
        out_shape=jax.ShapeDtypeStruct((A_HEADS, rows, win), F32),
        grid=(A_HEADS,),
        in_specs=[pl.BlockSpec((1, 1, win), lambda h: (h, 0, 0))],
        out_specs=pl.BlockSpec((1, rows, win), lambda h: (h, 0, 0)),
        compiler_params=_cparams(1),
        name="bias_sample",
    )(w)


def _stack_query(q, rows):
    lane = lax.broadcasted_iota(jnp.int32, (rows, HEAD_W), 1)
    zero = jnp.zeros_like(q)
    return jnp.concatenate([jnp.where(lane < A_QK_DIM, q, zero),
                            jnp.where(lane >= A_QK_DIM, q, zero)], axis=0)


def _attn_prompt_kernel(lam_ref, q_ref, k_ref, vt_ref, bias_ref, g_ref, o_ref,
                        qs_scr, s1_scr, sg_scr, m_scr, acc_scr, *, t, group, nh, lam_init):
    i = pl.program_id(2)
    tg = group * t
    heads = range(nh)
    for hh in heads:
        qs_scr[hh] = _stack_query(q_ref[0, :, hh * HEAD_W:(hh + 1) * HEAD_W], t)
    acc_scr[...] = jnp.zeros_like(acc_scr)
    m_scr[...] = jnp.full_like(m_scr, -jnp.inf)

    def scores(k0, n_keys):
        rows = pl.ds(pl.multiple_of(k0, t), n_keys)
        return [_dot_nt(k_ref[0, rows, hh * HEAD_W:(hh + 1) * HEAD_W], qs_scr[hh]) for hh in heads]

    def keep(s_scr, s, bias_sel=None):
        out = []
        for hh in heads:
            sh = s[hh] if bias_sel is None else s[hh] + bias_ref[hh, bias_sel]
            s_scr[hh] = sh
            out.append(jnp.max(sh, axis=0, keepdims=True))
        return tuple(out)

    def fold(s_scr, k0, n_keys, s_max):
        cols = pl.ds(pl.multiple_of(k0, t), n_keys)
        for hh in heads:
            m_prev = m_scr[hh]
            m_next = jnp.maximum(m_prev, s_max[hh])
            alpha = jnp.exp2(m_prev - m_next)
            p = jnp.exp2(s_scr[hh] - m_next).astype(BF16)
            vt = vt_ref[0, hh * VT_ROWS:(hh + 1) * VT_ROWS, cols]
            acc_scr[hh] = alpha * acc_scr[hh] + jnp.dot(vt, p, preferred_element_type=F32)
            m_scr[hh] = m_next

    n_near = jnp.where(i >= 1, 2 + lax.rem(i - 1, group), 1)
    n_groups = (i + 1 - n_near) // group

    def near_body(r, s_max):
        s_next = scores((i - r) * t, t)
        fold(s1_scr, (i - r + 1) * t, t, s_max)
        return keep(s1_scr, s_next, jnp.minimum(r, 2))

    s_max = keep(s1_scr, scores(i * t, t), 0)
    s_max = lax.fori_loop(1, n_near, near_body, s_max)
    s_far = scores(0, tg)
    fold(s1_scr, (i - n_near + 1) * t, t, s_max)
    s_max = keep(sg_scr, s_far)

    def far_body(g, s_max):
        s_next = scores(g * tg, tg)
        fold(sg_scr, (g - 1) * tg, tg, s_max)
        return keep(sg_scr, s_next)

    s_max = lax.fori_loop(1, n_groups, far_body, s_max)

    @pl.when(n_groups > 0)
    def _():
        fold(sg_scr, (n_groups - 1) * tg, tg, s_max)

    for hh in heads:
        acc = acc_scr[hh]
        o_all = acc[:HEAD_W] / acc[HEAD_W:HEAD_W + 1]
        o = o_all[:, :t] - lam_ref[0] * o_all[:, t:]
        o = o * lax.rsqrt(jnp.mean(o * o, axis=0, keepdims=True) + EPS) * (g_ref[...] * (1.0 - lam_init))
        o_ref[0, :, hh * HEAD_W:(hh + 1) * HEAD_W] = o.T.astype(BF16)


def _attn_prompt(lam, qa, ka, vt, bias_tiles, g_col, lam_init, t, group, nh):
    b, s, _ = qa.shape
    kernel = functools.partial(_attn_prompt_kernel, t=t, group=group, nh=nh, lam_init=lam_init)
    tok = pl.BlockSpec((1, t, nh * HEAD_W), lambda bi, h, i: (bi, i, h))
    return pl.pallas_call(
        kernel,
        out_shape=jax.ShapeDtypeStruct((b, s, GROUP_W), BF16),
        grid=(b, A_HEADS // nh, s // t),
        in_specs=[pl.BlockSpec(memory_space=pltpu.SMEM),
                  tok,
                  _resident((1, s, nh * HEAD_W), lambda bi, h, i: (bi, 0, h)),
                  _resident((1, nh * VT_ROWS, s), lambda bi, h, i: (bi, h, 0)),
                  _resident((nh, 3, t, 2 * t), lambda bi, h, i: (h, 0, 0, 0)),
                  pl.BlockSpec((HEAD_W, 1), lambda bi, h, i: (0, 0))],
        out_specs=tok,
        scratch_shapes=[pltpu.VMEM((nh, 2 * t, HEAD_W), BF16),
                        pltpu.VMEM((nh, t, 2 * t), F32),
                        pltpu.VMEM((nh, group * t, 2 * t), F32),
                        pltpu.VMEM((nh, 1, 2 * t), F32),
                        pltpu.VMEM((nh, VT_ROWS, 2 * t), F32)],
        compiler_params=_cparams(3),
        name="attn_prompt",
    )(lam, qa, ka, vt, bias_tiles, g_col)


def _attn_sample_kernel(lam_ref, q_ref, kc_ref, vc_ref, kn_ref, vn_ref, bc_ref, bn_ref, g_ref, o_ref,
                        *, rows, past, lam_init):
    lam = lam_ref[0]
    for hh in range(A_HEADS):
        sl = slice(hh * HEAD_W, (hh + 1) * HEAD_W)
        qs = _stack_query(q_ref[0, :, sl], rows)
        kc = kc_ref[0, pl.ds(hh, past, stride=A_HEADS), :].astype(BF16)
        vc = vc_ref[0, pl.ds(hh, past, stride=A_HEADS), :].astype(BF16)
        bc = bc_ref[hh]
        bn = bn_ref[hh]
        s_c = _dot_nt(qs, kc) + jnp.concatenate([bc, bc], axis=0)
        s_n = _dot_nt(qs, kn_ref[0, :, sl]) + jnp.concatenate([bn, bn], axis=0)
        m = jnp.maximum(jnp.max(s_c, axis=1, keepdims=True), jnp.max(s_n, axis=1, keepdims=True))
        p_c = jnp.exp2(s_c - m)
        p_n = jnp.exp2(s_n - m)
        inv_l = 1.0 / (jnp.sum(p_c, axis=1, keepdims=True) + jnp.sum(p_n, axis=1, keepdims=True))
        p_c = p_c * inv_l
        p_n = p_n * inv_l
        d_c = (p_c[:rows] - lam * p_c[rows:]).astype(BF16)
        d_n = (p_n[:rows] - lam * p_n[rows:]).astype(BF16)
        o = (jnp.dot(d_c, vc, preferred_element_type=F32)
             + jnp.dot(d_n, vn_ref[0, :, sl], preferred_element_type=F32))
        o_ref[0, :, sl] = (_rms(o) * g_ref[...] * (1.0 - lam_init)).astype(BF16)


def _attn_sample(lam, qa, cache_k, cache_v, ka, va, bias_c, bias_n, g_row, lam_init):
    nb, rows, _ = qa.shape
    past = cache_k.shape[1] // A_HEADS
    kernel = functools.partial(_attn_sample_kernel, rows=rows, past=past, lam_init=lam_init)
    new_spec = pl.BlockSpec((1, rows, GROUP_W), lambda bi: (bi, 0, 0))
    cache_spec = pl.BlockSpec((1, past * A_HEADS, HEAD_W), lambda bi: (bi, 0, 0))
    return pl.pallas_call(
        kernel,
        out_shape=jax.ShapeDtypeStruct((nb, rows, GROUP_W), BF16),
        grid=(nb,),
        in_specs=[pl.BlockSpec(memory_space=pltpu.SMEM),
                  new_spec, cache_spec, cache_spec, new_spec, new_spec,
                  pl.BlockSpec((A_HEADS, rows, past), lambda bi: (0, 0, 0)),
                  pl.BlockSpec((A_HEADS, rows, rows), lambda bi: (0, 0, 0)),
                  pl.BlockSpec((1, HEAD_W), lambda bi: (0, 0))],
        out_specs=new_spec,
        compiler_params=_cparams(1),
        name="attn_sample",
    )(lam, qa, cache_k, cache_v, ka, va, bias_c, bias_n, g_row)


def _retention_kernel(lg_ref, q_ref, k_ref, v_ref, g_ref, s0_ref, gsub_ref, o_ref, st_ref, *, c):
    @pl.when(pl.program_id(1) == 0)
    def _():
        st_ref[...] = s0_ref[...]

    row = lax.broadcasted_iota(jnp.int32, (c, c), 0)
    col = lax.broadcasted_iota(jnp.int32, (c, c), 1)
    diff = (row - col).astype(F32)
    causal = diff >= 0.0
    diff = jnp.maximum(diff, 0.0)
    idx = lax.broadcasted_iota(jnp.int32, (c, 1), 0).astype(F32)
    gsub = gsub_ref[...]

    for hh in range(R_HEADS):
        lg = lg_ref[hh]
        sl = slice(hh * HEAD_W, (hh + 1) * HEAD_W)
        q = q_ref[0, :, sl]
        k = k_ref[0, :, sl]
        v = v_ref[0, :, sl]
        decay = jnp.where(causal, jnp.exp(lg * diff), 0.0)
        scores = _dot_nt(q, k) * decay
        state = st_ref[0, hh]
        o = jnp.dot(scores.astype(BF16), v, preferred_element_type=F32)
        o = o + jnp.dot(q, state.astype(BF16), preferred_element_type=F32) * jnp.exp(lg * (idx + 1.0))
        kd = (k.astype(F32) * jnp.exp(lg * (c - 1.0 - idx))).astype(BF16)
        st_ref[0, hh] = jnp.exp(lg * c) * state + _dot_tn(kd, v)
        gate = g_ref[0, :, sl].astype(F32)
        o_ref[0, :, sl] = (_rms(o) * gsub * _silu(gate)).astype(BF16)


def _retention(log_gamma, qr, kr, vr, gr, state0, g_sub_r, c):
    b, s, _ = qr.shape
    kernel = functools.partial(_retention_kernel, c=c)
    tok = pl.BlockSpec((1, c, GROUP_W), lambda bi, i: (bi, i, 0))
    st = pl.BlockSpec((1, R_HEADS, HEAD_W, HEAD_W), lambda bi, i: (bi, 0, 0, 0))
    return pl.pallas_call(
        kernel,
        out_shape=(jax.ShapeDtypeStruct((b, s, GROUP_W), BF16),
                   jax.ShapeDtypeStruct((b, R_HEADS, HEAD_W, HEAD_W), F32)),
        grid=(b, s // c),
        in_specs=[pl.BlockSpec(memory_space=pltpu.SMEM), tok, tok, tok, tok, st,
                  pl.BlockSpec((1, HEAD_W), lambda bi, i: (0, 0))],
        out_specs=(tok, st),
        compiler_params=_cparams(2),
        name="retention",
    )(log_gamma, qr, kr, vr, gr, state0, g_sub_r)


def _mlp_kernel(x_ref, oa_ref, or_ref, mod_ref, wout_ref, gffn_ref, wup_ref, wconv_ref, bconv_ref,
                wdown_ref, gfin_ref, cprev_ref, y_ref, cst_ref, ue_scr, acc_scr, hb_scr, *, tm, shift):
    halo = (CONV_W - 1) * shift
    pad = -(-halo // 8) * 8

    @pl.when(pl.program_id(1) == 0)
    def _():
        cst_ref[...] = cprev_ref[...]

    mixed = (jnp.dot(oa_ref[0], wout_ref[:GROUP_W, :], preferred_element_type=F32)
             + jnp.dot(or_ref[0], wout_ref[GROUP_W:, :], preferred_element_type=F32))
    x1 = x_ref[0] + mod_ref[0, 2] * mixed
    y_ref[0] = x1
    h = _rms(x1) * gffn_ref[...]
    h = h * (1.0 + mod_ref[0, 4]) + mod_ref[0, 3]
    hb_scr[...] = h.astype(BF16)

    def up_half(slot, col0):
        cols = slice(col0, col0 + FF_CHUNK)
        ue_scr[slot, pad - halo:pad, :] = cst_ref[0, :, cols]
        ue_scr[slot, pad:pad + tm, :] = jnp.dot(hb_scr[...], wup_ref[:, cols], preferred_element_type=F32)
        cst_ref[0, :, cols] = ue_scr[slot, pad + tm - halo:pad + tm, :]

    def up(ch):
        up_half(2 * (ch % 2), ch * FF_CHUNK)
        up_half(2 * (ch % 2) + 1, D_FF + ch * FF_CHUNK)

    def conv_half(slot, col0):
        cols = slice(col0, col0 + FF_CHUNK)
        y = bconv_ref[:, cols]
        for j in range(CONV_W):
            start = pad - halo + j * shift
            y = y + wconv_ref[j:j + 1, cols] * ue_scr[slot, start:start + tm, :]
        return y

    n_chunks = D_FF // FF_CHUNK
    up(0)
    for ch in range(n_chunks):
        if ch + 1 < n_chunks:
            up(ch + 1)
        ya = conv_half(2 * (ch % 2), ch * FF_CHUNK)
        yg = conv_half(2 * (ch % 2) + 1, D_FF + ch * FF_CHUNK)
        act = (_silu(ya) * yg).astype(BF16)
        part = jnp.dot(act, wdown_ref[ch * FF_CHUNK:(ch + 1) * FF_CHUNK, :], preferred_element_type=F32)
        if ch == 0:
            acc_scr[...] = part
        else:
            acc_scr[...] += part

    x2 = y_ref[0] + mod_ref[0, 5] * acc_scr[...]
    y_ref[0] = _rms(x2) * gfin_ref[...]


def _mlp(x, oa, orr, mod, w_out_b, g_ffn, w_up_b, w_conv, b_conv, w_down_b, g_final, conv_prev, tm, shift):
    b, length, _ = x.shape
    l_mod = mod.shape[2]
    mod_blk = 1 if l_mod == 1 else tm
    halo = (CONV_W - 1) * shift
    pad = -(-halo // 8) * 8
    kernel = functools.partial(_mlp_kernel, tm=tm, shift=shift)
    tok = lambda bi, i: (bi, i, 0)
    const = lambda bi, i: (0, 0)
    return pl.pallas_call(
        kernel,
        out_shape=(jax.ShapeDtypeStruct((b, length, D_MODEL), F32),
                   jax.ShapeDtypeStruct((b, halo, 2 * D_FF), F32)),
        grid=(b, length // tm),
        in_specs=[pl.BlockSpec((1, tm, D_MODEL), tok),
                  pl.BlockSpec((1, tm, GROUP_W), tok),
                  pl.BlockSpec((1, tm, GROUP_W), tok),
                  pl.BlockSpec((1, 6, mod_blk, D_MODEL),
                               (lambda bi, i: (bi, 0, 0, 0)) if l_mod == 1 else (lambda bi, i: (bi, 0, i, 0))),
                  _resident((2 * GROUP_W, D_MODEL), const),
                  pl.BlockSpec((1, D_MODEL), const),
                  _resident((D_MODEL, 2 * D_FF), const),
                  pl.BlockSpec((CONV_W, 2 * D_FF), const),
                  pl.BlockSpec((1, 2 * D_FF), const),
                  _resident((D_FF, D_MODEL), const),
                  pl.BlockSpec((1, D_MODEL), const),
                  pl.BlockSpec((1, halo, 2 * D_FF), lambda bi, i: (bi, 0, 0))],
        out_specs=(pl.BlockSpec((1, tm, D_MODEL), tok),
                   pl.BlockSpec((1, halo, 2 * D_FF), lambda bi, i: (bi, 0, 0))),
        scratch_shapes=[pltpu.VMEM((4, pad + tm, FF_CHUNK), F32),
                        pltpu.VMEM((tm, D_MODEL), F32),
                        pltpu.VMEM((tm, D_MODEL), BF16)],
        compiler_params=_cparams(2),
        name="mlp",
    )(x, oa, orr, mod, w_out_b, g_ffn, w_up_b, w_conv, b_conv, w_down_b, g_final, conv_prev)


def _rotary_tables(pos):
    half = HEAD_W // 2
    inv_freq = ROPE_BASE ** (-jnp.arange(half, dtype=F32) / half)
    ang = pos.astype(F32)[:, None] * inv_freq[None, :]
    cos = jnp.cos(ang)
    sin = jnp.sin(ang)
    return jnp.concatenate([cos, cos], axis=-1), jnp.concatenate([-sin, sin], axis=-1)


def kernel(x_prompt, x_sample, cache_k, cache_v, state_ret, state_conv, c_prompt, c_sample,
           w_ada, b_ada, g_mix, w_in, lambda_q1, lambda_k1, lambda_q2, lambda_k2,
           g_sub_a, g_sub_r, w_out, g_ffn, w_up, w_conv, b_conv, w_down, rel_bias, g_final):
    depth = w_in.shape[0]
    assert depth == 1
    l = 0
    nb_p, s_p, _ = x_prompt.shape
    nb_s, s_new, _ = x_sample.shape
    past = cache_k.shape[2]
    n_tok_s = nb_s * s_new

    log_gamma = jnp.log(1.0 - 2.0 ** (-5.0 - jnp.arange(R_HEADS, dtype=F32)))
    lam_init = 0.8 - 0.6 * math.exp(-0.3 * l)
    lam = (jnp.exp(jnp.sum(lambda_q1[l].astype(F32) * lambda_k1[l].astype(F32)))
           - jnp.exp(jnp.sum(lambda_q2[l].astype(F32) * lambda_k2[l].astype(F32))) + lam_init).reshape(1)

    w_in_b = w_in[l].astype(BF16)
    w_out_b = w_out[l].astype(BF16)
    w_up_b = w_up[l].astype(BF16)
    w_down_b = w_down[l].astype(BF16)
    g_mix_l = g_mix[l].reshape(1, D_MODEL)
    g_ffn_l = g_ffn[l].reshape(1, D_MODEL)
    g_fin = g_final.reshape(1, D_MODEL)
    g_sa_row = g_sub_a[l].reshape(1, HEAD_W)
    g_sa_col = g_sub_a[l].reshape(HEAD_W, 1)
    g_sr = g_sub_r[l].reshape(1, HEAD_W)
    w_conv_l = w_conv[l]
    b_conv_l = b_conv[l].reshape(1, 2 * D_FF)

    mods = _adaln(jnp.concatenate([c_prompt, c_sample], axis=0), w_ada[l], b_ada[l])
    mods = mods.reshape(nb_p + nb_s, 6, D_MODEL)
    mod_p = mods[:nb_p].reshape(nb_p, 6, 1, D_MODEL)
    mod_s = jnp.transpose(mods[nb_p:], (1, 0, 2))
    mod_s_stream = jnp.repeat(mod_s, s_new, axis=1)[None]
    mod_s_time = jnp.tile(mod_s, (1, s_new, 1))[None]

    t_attn = 256
    cos_p, sin_p = _rotary_tables(jnp.arange(s_p, dtype=jnp.int32))
    (k32_p, v32_p, qa_p, ka_p, _, qr_p, kr_p, vr_p, gr_p, vt_p) = _proj(
        x_prompt, mod_p, g_mix_l, w_in_b, cos_p, sin_p, tm=512, with_vt=True)

    assert 2 * t_attn - 1 >= MAX_DISTANCE and t_attn % CHUNK == 0
    lanes = jnp.arange(2 * t_attn, dtype=jnp.int32)
    w_p = _bias_vector(rel_bias, jnp.concatenate([t_attn - lanes, -lanes])).reshape(A_HEADS, 2, 2 * t_attn)
    bias_tiles = _bias_prompt(w_p, t_attn)
    oa_p = _attn_prompt(lam, qa_p, ka_p, vt_p, bias_tiles, g_sa_col, lam_init, t_attn, group=4, nh=4)

    or_p, ret_p = _retention(log_gamma, qr_p, kr_p, vr_p, gr_p,
                             jnp.zeros((nb_p, R_HEADS, HEAD_W, HEAD_W), F32), g_sr, c=256)

    y_p, conv_p = _mlp(x_prompt, oa_p, or_p, mod_p, w_out_b, g_ffn_l, w_up_b, w_conv_l, b_conv_l,
                       w_down_b, g_fin, jnp.zeros((nb_p, CONV_W - 1, 2 * D_FF), F32), tm=512, shift=1)

    pos_s = past + jnp.arange(s_new, dtype=jnp.int32)
    cos_s, sin_s = _rotary_tables(pos_s)
    cos_s = jnp.tile(cos_s, (nb_s, 1))
    sin_s = jnp.tile(sin_s, (nb_s, 1))
    (k32_s, v32_s, qa_s, ka_s, va_s, qr_s, kr_s, vr_s, gr_s) = _proj(
        x_sample.reshape(1, n_tok_s, D_MODEL), mod_s_stream, g_mix_l, w_in_b, cos_s, sin_s, tm=n_tok_s)

    def streams(t):
        return t.reshape(nb_s, s_new, GROUP_W)

    near, win = 256, 512
    assert near >= MAX_DISTANCE and near + s_new <= win - s_new and past >= near
    k0 = past - near
    w_s = _bias_vector(rel_bias, k0 + jnp.arange(win, dtype=jnp.int32) - past).reshape(A_HEADS, 1, win)
    bias_s = _bias_sample(w_s, s_new, past, k0)
    bias_c = jnp.concatenate([jnp.broadcast_to(bias_s[:, :, :1], (A_HEADS, s_new, k0)),
                              bias_s[:, :, :near]], axis=-1)
    bias_n = bias_s[:, :, near:near + s_new]
    oa_s = _attn_sample(lam, streams(qa_s), cache_k[l].reshape(nb_s, past * A_HEADS, HEAD_W),
                        cache_v[l].reshape(nb_s, past * A_HEADS, HEAD_W), streams(ka_s), streams(va_s),
                        bias_c, bias_n, g_sa_row, lam_init)

    or_s, ret_s = _retention(log_gamma, streams(qr_s), streams(kr_s), streams(vr_s), streams(gr_s),
                             state_ret[l].astype(F32), g_sr, c=s_new)

    def time_major(t):
        return jnp.transpose(t, (1, 0, 2)).reshape(1, n_tok_s, t.shape[-1])

    conv_prev_s = jnp.transpose(state_conv[l], (1, 0, 2)).reshape(1, (CONV_W - 1) * nb_s, 2 * D_FF)
    y_s, conv_s = _mlp(time_major(x_sample), time_major(oa_s), time_major(or_s), mod_s_time,
                       w_out_b, g_ffn_l, w_up_b, w_conv_l, b_conv_l, w_down_b, g_fin, conv_prev_s,
                       tm=n_tok_s, shift=nb_s)
    y_s = jnp.transpose(y_s.reshape(s_new, nb_s, D_MODEL), (1, 0, 2))
    conv_s = jnp.transpose(conv_s.reshape(CONV_W - 1, nb_s, 2 * D_FF), (1, 0, 2))

    def heads(t, n):
        return t.reshape(1, n, -1, A_HEADS, HEAD_W)

    return (y_p, y_s, heads(k32_p, nb_p), heads(v32_p, nb_p), ret_p[None], conv_p[None],
            heads(k32_s, nb_s), heads(v32_s, nb_s), ret_s[None], conv_s[None])
```

```python
import functools
import math

import jax
import jax.numpy as jnp
from jax import lax
from jax.experimental import pallas as pl
from jax.experimental.pallas import tpu as pltpu

F32 = jnp.float32
BF16 = jnp.bfloat16

D_MODEL = 1024
CHUNK = 64
A_HEADS = 4
A_QK_DIM = 64
HEAD_W = 128
GROUP_W = A_HEADS * HEAD_W
N_GROUPS = 7
R_HEADS = 4
D_FF = 2816
CONV_W = 3
N_BUCKETS = 32
FAR_BUCKET = N_BUCKETS // 2 - 1
MAX_DISTANCE = 128
ROPE_BASE = 10000.0
EPS = 1e-6
NEG_INF = -1e30

VMEM_LIMIT = 56 * 1024 * 1024
FF_CHUNK = 256
LOG2E = math.log2(math.e)
Q_SCALE = A_QK_DIM ** -0.5 * LOG2E
VT_ROWS = HEAD_W + 16


def _cparams(n_axes, flags=None):
    return pltpu.CompilerParams(dimension_semantics=("arbitrary",) * n_axes,
                                vmem_limit_bytes=VMEM_LIMIT, flags=flags)


def _resident(block_shape, index_map):
    return pl.BlockSpec(block_shape, index_map, pipeline_mode=pl.Buffered(1))


def _rms(x):
    return x * lax.rsqrt(jnp.mean(x * x, axis=-1, keepdims=True) + EPS)


def _silu(x):
    return x * jax.nn.sigmoid(x)


def _dot_nt(a, b):
    return lax.dot_general(a, b, (((1,), (1,)), ((), ())), preferred_element_type=F32)


def _dot_tn(a, b):
    return lax.dot_general(a, b, (((0,), (0,)), ((), ())), preferred_element_type=F32)


def _adaln_kernel(c_ref, w_ref, b_ref, o_ref):
    a = _silu(c_ref[...]).astype(BF16)
    o_ref[...] = jnp.dot(a, w_ref[...].astype(BF16), preferred_element_type=F32) + b_ref[...]


def _adaln(c, w_ada, b_ada):
    n_rows = c.shape[0]
    n_cols = w_ada.shape[1]
    tn = 1536
    return pl.pallas_call(
        _adaln_kernel,
        out_shape=jax.ShapeDtypeStruct((n_rows, n_cols), F32),
        grid=(n_cols // tn,),
        in_specs=[pl.BlockSpec((n_rows, D_MODEL), lambda j: (0, 0)),
                  pl.BlockSpec((D_MODEL, tn), lambda j: (0, j)),
                  pl.BlockSpec((1, tn), lambda j: (0, j))],
        out_specs=pl.BlockSpec((n_rows, tn), lambda j: (0, j)),
        compiler_params=_cparams(1),
        name="adaln",
    )(c, w_ada, b_ada.reshape(1, n_cols))


def _proj_kernel(x_ref, mod_ref, g_ref, w_ref, cos_ref, sin_ref,
                 k32_ref, v32_ref, qa_ref, ka_ref, va_ref, qr_ref, kr_ref, vr_ref, gr_ref, *maybe_vt_ref,
                 tm):
    x = x_ref[0]
    h = _rms(x) * g_ref[...]
    h = h * (1.0 + mod_ref[0, 1]) + mod_ref[0, 0]
    hb = h.astype(BF16)

    def group(g):
        return jnp.dot(hb, w_ref[:, g * GROUP_W:(g + 1) * GROUP_W], preferred_element_type=F32)

    def store_heads(ref, t):
        for hh in range(A_HEADS):
            ref[0, pl.ds(hh, tm, stride=A_HEADS), :] = t[:, hh * HEAD_W:(hh + 1) * HEAD_W]

    qa_ref[0] = (group(0) * Q_SCALE).astype(BF16)
    ka = group(1)
    store_heads(k32_ref, ka)
    ka_ref[0] = ka.astype(BF16)
    va = group(2)
    store_heads(v32_ref, va)
    va_ref[0] = va.astype(BF16)
    if maybe_vt_ref:
        vt_ref = maybe_vt_ref[0]
        vt = va.T.astype(BF16)
        ones = jnp.ones((VT_ROWS - HEAD_W, tm), BF16)
        for hh in range(A_HEADS):
            vt_ref[0, hh * VT_ROWS:hh * VT_ROWS + HEAD_W, :] = vt[hh * HEAD_W:(hh + 1) * HEAD_W, :]
            vt_ref[0, hh * VT_ROWS + HEAD_W:(hh + 1) * VT_ROWS, :] = ones

    cos = cos_ref[...]
    sin = sin_ref[...]

    def rotary(t, out_ref, scale):
        for hh in range(R_HEADS):
            th = t[:, hh * HEAD_W:(hh + 1) * HEAD_W]
            r = th * cos + pltpu.roll(th, HEAD_W // 2, 1) * sin
            if scale != 1.0:
                r = r * scale
            out_ref[0, :, hh * HEAD_W:(hh + 1) * HEAD_W] = r.astype(BF16)

    rotary(group(3), qr_ref, 1.0)
    rotary(group(4), kr_ref, HEAD_W ** -0.5)
    vr_ref[0] = group(5).astype(BF16)
    gr_ref[0] = group(6).astype(BF16)


def _proj(x, mod, g_mix, w_in_b, cos_t, sin_t, tm, with_vt=False):
    b, length, _ = x.shape
    l_mod = mod.shape[2]
    mod_blk = 1 if l_mod == 1 else tm
    tok = lambda bi, i: (bi, i, 0)
    out_tok = pl.BlockSpec((1, tm, GROUP_W), tok)
    out_heads = pl.BlockSpec((1, tm * A_HEADS, HEAD_W), tok)
    heads_out = jax.ShapeDtypeStruct((b, length * A_HEADS, HEAD_W), F32)
    bf_out = jax.ShapeDtypeStruct((b, length, GROUP_W), BF16)
    out_shape = (heads_out, heads_out) + (bf_out,) * 7
    out_specs = (out_heads, out_heads) + (out_tok,) * 7
    if with_vt:
        out_shape += (jax.ShapeDtypeStruct((b, A_HEADS * VT_ROWS, length), BF16),)
        out_specs += (pl.BlockSpec((1, A_HEADS * VT_ROWS, tm), lambda bi, i: (bi, 0, i)),)
    return pl.pallas_call(
        functools.partial(_proj_kernel, tm=tm),
        out_shape=out_shape,
        grid=(b, length // tm),
        in_specs=[pl.BlockSpec((1, tm, D_MODEL), tok),
                  pl.BlockSpec((1, 6, mod_blk, D_MODEL),
                               (lambda bi, i: (bi, 0, 0, 0)) if l_mod == 1 else (lambda bi, i: (bi, 0, i, 0))),
                  pl.BlockSpec((1, D_MODEL), lambda bi, i: (0, 0)),
                  _resident((D_MODEL, N_GROUPS * GROUP_W), lambda bi, i: (0, 0)),
                  pl.BlockSpec((tm, HEAD_W), lambda bi, i: (i, 0)),
                  pl.BlockSpec((tm, HEAD_W), lambda bi, i: (i, 0))],
        out_specs=out_specs,
        compiler_params=_cparams(2),
        name="proj",
    )(x, mod, g_mix, w_in_b, cos_t, sin_t)


def _t5_bucket(rel):
    half = N_BUCKETS // 2
    max_exact = half // 2
    ret = jnp.where(rel > 0, half, 0)
    n = jnp.abs(rel)
    large = max_exact + (jnp.log(jnp.maximum(n, 1).astype(jnp.float32) / max_exact)
                         / math.log(MAX_DISTANCE / max_exact) * (half - max_exact)).astype(jnp.int32)
    large = jnp.minimum(large, half - 1)
    return ret + jnp.where(n < max_exact, n, large)


def _bias_vector(rel_bias, rel):
    return jnp.transpose(rel_bias[_t5_bucket(rel)].astype(F32))


def _bias_prompt_kernel(w_ref, o_ref, *, t):
    far = w_ref[0, 1:2, 2 * t - 1:2 * t]
    k_pos = lax.broadcasted_iota(jnp.int32, (t, t), 0)
    q_pos = lax.broadcasted_iota(jnp.int32, (t, t), 1)
    visible = (k_pos // CHUNK) <= (q_pos // CHUNK)
    for a in range(2):
        x = jnp.broadcast_to(w_ref[0, a:a + 1, :], (t, 2 * t))
        r = pltpu.roll(x, t, 1, stride=1, stride_axis=0)
        val = (r[:, :t] - far) * LOG2E
        if a == 0:
            val = jnp.where(visible, val, NEG_INF)
        o_ref[0, a] = jnp.concatenate([val, val], axis=1)
    o_ref[0, 2] = jnp.zeros((t, 2 * t), F32)


def _bias_prompt(w, t):
    return pl.pallas_call(
        functools.partial(_bias_prompt_kernel, t=t),
        out_shape=jax.ShapeDtypeStruct((A_HEADS, 3, t, 2 * t), F32),
        grid=(A_HEADS,),
        in_specs=[pl.BlockSpec((1, 2, 2 * t), lambda h: (h, 0, 0))],
        out_specs=pl.BlockSpec((1, 3, t, 2 * t), lambda h: (h, 0, 0, 0)),
        compiler_params=_cparams(1),
        name="bias_prompt",
    )(w)


def _bias_sample_kernel(w_ref, o_ref, *, rows, win, past, k0):
    x = jnp.broadcast_to(w_ref[0], (rows, win))
    r = pltpu.roll(x, 0, 1, stride=1, stride_axis=0)
    row = lax.broadcasted_iota(jnp.int32, (rows, win), 0)
    col = lax.broadcasted_iota(jnp.int32, (rows, win), 1)
    val = jnp.where(col >= row, r, w_ref[0, :, 0:1]) * LOG2E
    visible = ((k0 + col) // CHUNK) <= ((past + row) // CHUNK)
    o_ref[0] = jnp.where(visible, val, NEG_INF)


def _bias_sample(w, rows, past, k0):
    win = w.shape[-1]
    return pl.pallas_call(
        functools.partial(_bias_sample_kernel, rows=rows, win=win, past=past, k0=k0),
        out_shape=jax.ShapeDtypeStruct((A_HEADS, rows, win), F32),
        grid=(A_HEADS,),
        in_specs=[pl.BlockSpec((1, 1, win), lambda h: (h, 0, 0))],
        out_specs=pl.BlockSpec((1, rows, win), lambda h: (h, 0, 0)),
        compiler_params=_cparams(1),
        name="bias_sample",
    )(w)


def _stack_query(q, rows):
    lane = lax.broadcasted_iota(jnp.int32, (rows, HEAD_W), 1)
    zero = jnp.zeros_like(q)
    return jnp.concatenate([jnp.where(lane < A_QK_DIM, q, zero),
                            jnp.where(lane >= A_QK_DIM, q, zero)], axis=0)


def _attn_prompt_kernel(lam_ref, q_ref, k_ref, vt_ref, bias_ref, g_ref, o_ref,
                        qs_scr, s1_scr, sg_scr, m_scr, acc_scr, *, t, group, nh, lam_init):
    i = pl.program_id(2)
    tg = group * t
    heads = range(nh)
    for hh in heads:
        qs_scr[hh] = _stack_query(q_ref[0, :, hh * HEAD_W:(hh + 1) * HEAD_W], t)
    acc_scr[...] = jnp.zeros_like(acc_scr)
    m_scr[...] = jnp.full_like(m_scr, -jnp.inf)

    def scores(k0, n_keys):
        rows = pl.ds(pl.multiple_of(k0, t), n_keys)
        return [_dot_nt(k_ref[0, rows, hh * HEAD_W:(hh + 1) * HEAD_W], qs_scr[hh]) for hh in heads]

    def keep(s_scr, s, bias_sel=None):
        out = []
        for hh in heads:
            sh = s[hh] if bias_sel is None else s[hh] + bias_ref[hh, bias_sel]
            s_scr[hh] = sh
            out.append(jnp.max(sh, axis=0, keepdims=True))
        return tuple(out)

    def fold(s_scr, k0, n_keys, s_max):
        cols = pl.ds(pl.multiple_of(k0, t), n_keys)
        for hh in heads:
            m_prev = m_scr[hh]
            m_next = jnp.maximum(m_prev, s_max[hh])
            alpha = jnp.exp2(m_prev - m_next)
            p = jnp.exp2(s_scr[hh] - m_next).astype(BF16)
            vt = vt_ref[0, hh * VT_ROWS:(hh + 1) * VT_ROWS, cols]
            acc_scr[hh] = alpha * acc_scr[hh] + jnp.dot(vt, p, preferred_element_type=F32)
            m_scr[hh] = m_next

    n_near = jnp.where(i >= 1, 2 + lax.rem(i - 1, group), 1)
    n_groups = (i + 1 - n_near) // group

    def near_body(r, s_max):
        s_next = scores((i - r) * t, t)
        fold(s1_scr, (i - r + 1) * t, t, s_max)
        return keep(s1_scr, s_next, jnp.minimum(r, 2))

    s_max = keep(s1_scr, scores(i * t, t), 0)
    s_max = lax.fori_loop(1, n_near, near_body, s_max)
    s_far = scores(0, tg)
    fold(s1_scr, (i - n_near + 1) * t, t, s_max)
    s_max = keep(sg_scr, s_far)

    def far_body(g, s_max):
        s_next = scores(g * tg, tg)
        fold(sg_scr, (g - 1) * tg, tg, s_max)
        return keep(sg_scr, s_next)

    s_max = lax.fori_loop(1, n_groups, far_body, s_max)

    @pl.when(n_groups > 0)
    def _():
        fold(sg_scr, (n_groups - 1) * tg, tg, s_max)

    for hh in heads:
        acc = acc_scr[hh]
        o_all = acc[:HEAD_W] / acc[HEAD_W:HEAD_W + 1]
        o = o_all[:, :t] - lam_ref[0] * o_all[:, t:]
        o = o * lax.rsqrt(jnp.mean(o * o, axis=0, keepdims=True) + EPS) * (g_ref[...] * (1.0 - lam_init))
        o_ref[0, :, hh * HEAD_W:(hh + 1) * HEAD_W] = o.T.astype(BF16)


def _attn_prompt(lam, qa, ka, vt, bias_tiles, g_col, lam_init, t, group, nh):
    b, s, _ = qa.shape
    kernel = functools.partial(_attn_prompt_kernel, t=t, group=group, nh=nh, lam_init=lam_init)
    tok = pl.BlockSpec((1, t, nh * HEAD_W), lambda bi, h, i: (bi, i, h))
    return pl.pallas_call(
        kernel,
        out_shape=jax.ShapeDtypeStruct((b, s, GROUP_W), BF16),
        grid=(b, A_HEADS // nh, s // t),
        in_specs=[pl.BlockSpec(memory_space=pltpu.SMEM),
                  tok,
                  _resident((1, s, nh * HEAD_W), lambda bi, h, i: (bi, 0, h)),
                  _resident((1, nh * VT_ROWS, s), lambda bi, h, i: (bi, h, 0)),
                  _resident((nh, 3, t, 2 * t), lambda bi, h, i: (h, 0, 0, 0)),
                  pl.BlockSpec((HEAD_W, 1), lambda bi, h, i: (0, 0))],
        out_specs=tok,
        scratch_shapes=[pltpu.VMEM((nh, 2 * t, HEAD_W), BF16),
                        pltpu.VMEM((nh, t, 2 * t), F32),
                        pltpu.VMEM((nh, group * t, 2 * t), F32),
                        pltpu.VMEM((nh, 1, 2 * t), F32),
                        pltpu.VMEM((nh, VT_ROWS, 2 * t), F32)],
        compiler_params=_cparams(3),
        name="attn_prompt",
    )(lam, qa, ka, vt, bias_tiles, g_col)


def _attn_sample_kernel(lam_ref, q_ref, kc_ref, vc_ref, kn_ref, vn_ref, bc_ref, bn_ref, g_ref, o_ref,
                        *, rows, past, lam_init):
    lam = lam_ref[0]
    for hh in range(A_HEADS):
        sl = slice(hh * HEAD_W, (hh + 1) * HEAD_W)
        qs = _stack_query(q_ref[0, :, sl], rows)
        kc = kc_ref[0, pl.ds(hh, past, stride=A_HEADS), :].astype(BF16)
        vc = vc_ref[0, pl.ds(hh, past, stride=A_HEADS), :].astype(BF16)
        bc = bc_ref[hh]
        bn = bn_ref[hh]
        s_c = _dot_nt(qs, kc) + jnp.concatenate([bc, bc], axis=0)
        s_n = _dot_nt(qs, kn_ref[0, :, sl]) + jnp.concatenate([bn, bn], axis=0)
        m = jnp.maximum(jnp.max(s_c, axis=1, keepdims=True), jnp.max(s_n, axis=1, keepdims=True))
        p_c = jnp.exp2(s_c - m)
        p_n = jnp.exp2(s_n - m)
        inv_l = 1.0 / (jnp.sum(p_c, axis=1, keepdims=True) + jnp.sum(p_n, axis=1, keepdims=True))
        p_c = p_c * inv_l
        p_n = p_n * inv_l
        d_c = (p_c[:rows] - lam * p_c[rows:]).astype(BF16)
        d_n = (p_n[:rows] - lam * p_n[rows:]).astype(BF16)
        o = (jnp.dot(d_c, vc, preferred_element_type=F32)
             + jnp.dot(d_n, vn_ref[0, :, sl], preferred_element_type=F32))
        o_ref[0, :, sl] = (_rms(o) * g_ref[...] * (1.0 - lam_init)).astype(BF16)


def _attn_sample(lam, qa, cache_k, cache_v, ka, va, bias_c, bias_n, g_row, lam_init):
    nb, rows, _ = qa.shape
    past = cache_k.shape[1] // A_HEADS
    kernel = functools.partial(_attn_sample_kernel, rows=rows, past=past, lam_init=lam_init)
    new_spec = pl.BlockSpec((1, rows, GROUP_W), lambda bi: (bi, 0, 0))
    cache_spec = pl.BlockSpec((1, past * A_HEADS, HEAD_W), lambda bi: (bi, 0, 0))
    return pl.pallas_call(
        kernel,
        out_shape=jax.ShapeDtypeStruct((nb, rows, GROUP_W), BF16),
        grid=(nb,),
        in_specs=[pl.BlockSpec(memory_space=pltpu.SMEM),
                  new_spec, cache_spec, cache_spec, new_spec, new_spec,
                  pl.BlockSpec((A_HEADS, rows, past), lambda bi: (0, 0, 0)),
                  pl.BlockSpec((A_HEADS, rows, rows), lambda bi: (0, 0, 0)),
                  pl.BlockSpec((1, HEAD_W), lambda bi: (0, 0))],
        out_specs=new_spec,
        compiler_params=_cparams(1),
        name="attn_sample",
    )(lam, qa, cache_k, cache_v, ka, va, bias_c, bias_n, g_row)


def _retention_kernel(lg_ref, q_ref, k_ref, v_ref, g_ref, s0_ref, gsub_ref, o_ref, st_ref, *, c, nb):
    @pl.when(pl.program_id(1) == 0)
    def _():
        st_ref[...] = s0_ref[...]

    row = lax.broadcasted_iota(jnp.int32, (c, c), 0)
    col = lax.broadcasted_iota(jnp.int32, (c, c), 1)
    diff = (row - col).astype(F32)
    causal = diff >= 0.0
    diff = jnp.maximum(diff, 0.0)
    idx = lax.broadcasted_iota(jnp.int32, (c, 1), 0).astype(F32)
    gsub = gsub_ref[...]
    chains = [(bb, hh) for hh in range(R_HEADS) for bb in range(nb)]
    cols = lambda hh: slice(hh * HEAD_W, (hh + 1) * HEAD_W)

    qk, qs = {}, {}
    for bb, hh in chains:
        q = q_ref[bb, :, cols(hh)]
        qk[bb, hh] = _dot_nt(q, k_ref[bb, :, cols(hh)])
        qs[bb, hh] = jnp.dot(q, st_ref[bb, hh].astype(BF16), preferred_element_type=F32)

    for hh in range(R_HEADS):
        lg = lg_ref[hh]
        decay = jnp.where(causal, jnp.exp(lg * diff), 0.0)
        q_decay = jnp.exp(lg * (idx + 1.0))
        k_decay = jnp.exp(lg * (c - 1.0 - idx))
        for bb in range(nb):
            k = k_ref[bb, :, cols(hh)]
            v = v_ref[bb, :, cols(hh)]
            scores = qk[bb, hh] * decay
            o = jnp.dot(scores.astype(BF16), v, preferred_element_type=F32) + qs[bb, hh] * q_decay
            kd = (k.astype(F32) * k_decay).astype(BF16)
            st_ref[bb, hh] = jnp.exp(lg * c) * st_ref[bb, hh] + _dot_tn(kd, v)
            gate = g_ref[bb, :, cols(hh)].astype(F32)
            o_ref[bb, :, cols(hh)] = (_rms(o) * gsub * _silu(gate)).astype(BF16)


def _retention(log_gamma, qr, kr, vr, gr, state0, g_sub_r, c, nb):
    b, s, _ = qr.shape
    kernel = functools.partial(_retention_kernel, c=c, nb=nb)
    tok = pl.BlockSpec((nb, c, GROUP_W), lambda bi, i: (bi, i, 0))
    st = pl.BlockSpec((nb, R_HEADS, HEAD_W, HEAD_W), lambda bi, i: (bi, 0, 0, 0))
    return pl.pallas_call(
        kernel,
        out_shape=(jax.ShapeDtypeStruct((b, s, GROUP_W), BF16),
                   jax.ShapeDtypeStruct((b, R_HEADS, HEAD_W, HEAD_W), F32)),
        grid=(b // nb, s // c),
        in_specs=[pl.BlockSpec(memory_space=pltpu.SMEM), tok, tok, tok, tok, st,
                  pl.BlockSpec((1, HEAD_W), lambda bi, i: (0, 0))],
        out_specs=(tok, st),
        compiler_params=_cparams(2),
        name="retention",
    )(log_gamma, qr, kr, vr, gr, state0, g_sub_r)


def _mlp_kernel(x_ref, oa_ref, or_ref, mod_ref, wout_ref, gffn_ref, wup_ref, wconv_ref, bconv_ref,
                wdown_ref, gfin_ref, cprev_ref, y_ref, cst_ref, ue0, ue1, ue2, ue3, acc_scr, hb_scr,
                *, tm, shift):
    halo = (CONV_W - 1) * shift
    pad = -(-halo // 8) * 8
    ue_scr = (ue0, ue1, ue2, ue3)

    @pl.when(pl.program_id(1) == 0)
    def _():
        cst_ref[...] = cprev_ref[...]

    mixed = (jnp.dot(oa_ref[0], wout_ref[:GROUP_W, :], preferred_element_type=F32)
             + jnp.dot(or_ref[0], wout_ref[GROUP_W:, :], preferred_element_type=F32))
    x1 = x_ref[0] + mod_ref[0, 2] * mixed
    y_ref[0] = x1
    h = _rms(x1) * gffn_ref[...]
    h = h * (1.0 + mod_ref[0, 4]) + mod_ref[0, 3]
    hb_scr[...] = h.astype(BF16)

    def up_half(slot, col0):
        cols = slice(col0, col0 + FF_CHUNK)
        ue_scr[slot][pad - halo:pad, :] = cst_ref[0, :, cols]
        ue_scr[slot][pad:pad + tm, :] = jnp.dot(hb_scr[...], wup_ref[:, cols], preferred_element_type=F32)
        cst_ref[0, :, cols] = ue_scr[slot][pad + tm - halo:pad + tm, :]

    def up(ch):
        up_half(2 * (ch % 2), ch * FF_CHUNK)
        up_half(2 * (ch % 2) + 1, D_FF + ch * FF_CHUNK)

    def conv_half(slot, col0):
        cols = slice(col0, col0 + FF_CHUNK)
        y = bconv_ref[:, cols]
        for j in range(CONV_W):
            start = pad - halo + j * shift
            y = y + wconv_ref[j:j + 1, cols] * ue_scr[slot][start:start + tm, :]
        return y

    n_chunks = D_FF // FF_CHUNK
    up(0)
    for ch in range(n_chunks):
        nxt = ch + 1
        if nxt < n_chunks:
            up_half(2 * (nxt % 2), nxt * FF_CHUNK)
        ya = conv_half(2 * (ch % 2), ch * FF_CHUNK)
        if nxt < n_chunks:
            up_half(2 * (nxt % 2) + 1, D_FF + nxt * FF_CHUNK)
        yg = conv_half(2 * (ch % 2) + 1, D_FF + ch * FF_CHUNK)
        act = (_silu(ya) * yg).astype(BF16)
        part = jnp.dot(act, wdown_ref[ch * FF_CHUNK:(ch + 1) * FF_CHUNK, :], preferred_element_type=F32)
        if ch == 0:
            acc_scr[...] = part
        else:
            acc_scr[...] += part

    x2 = y_ref[0] + mod_ref[0, 5] * acc_scr[...]
    y_ref[0] = _rms(x2) * gfin_ref[...]


def _mlp(x, oa, orr, mod, w_out_b, g_ffn, w_up_b, w_conv, b_conv, w_down_b, g_final, conv_prev, tm, shift):
    b, length, _ = x.shape
    l_mod = mod.shape[2]
    mod_blk = 1 if l_mod == 1 else tm
    halo = (CONV_W - 1) * shift
    pad = -(-halo // 8) * 8
    kernel = functools.partial(_mlp_kernel, tm=tm, shift=shift)
    tok = lambda bi, i: (bi, i, 0)
    const = lambda bi, i: (0, 0)
    return pl.pallas_call(
        kernel,
        out_shape=(jax.ShapeDtypeStruct((b, length, D_MODEL), F32),
                   jax.ShapeDtypeStruct((b, halo, 2 * D_FF), F32)),
        grid=(b, length // tm),
        in_specs=[pl.BlockSpec((1, tm, D_MODEL), tok),
                  pl.BlockSpec((1, tm, GROUP_W), tok),
                  pl.BlockSpec((1, tm, GROUP_W), tok),
                  pl.BlockSpec((1, 6, mod_blk, D_MODEL),
                               (lambda bi, i: (bi, 0, 0, 0)) if l_mod == 1 else (lambda bi, i: (bi, 0, i, 0))),
                  _resident((2 * GROUP_W, D_MODEL), const),
                  pl.BlockSpec((1, D_MODEL), const),
                  _resident((D_MODEL, 2 * D_FF), const),
                  pl.BlockSpec((CONV_W, 2 * D_FF), const),
                  pl.BlockSpec((1, 2 * D_FF), const),
                  _resident((D_FF, D_MODEL), const),
                  pl.BlockSpec((1, D_MODEL), const),
                  pl.BlockSpec((1, halo, 2 * D_FF), lambda bi, i: (bi, 0, 0))],
        out_specs=(pl.BlockSpec((1, tm, D_MODEL), tok),
                   pl.BlockSpec((1, halo, 2 * D_FF), lambda bi, i: (bi, 0, 0))),
        scratch_shapes=[pltpu.VMEM((pad + tm, FF_CHUNK), F32)] * 4 + [
                        pltpu.VMEM((tm, D_MODEL), F32),
                        pltpu.VMEM((tm, D_MODEL), BF16)],
        compiler_params=_cparams(2),
        name="mlp",
    )(x, oa, orr, mod, w_out_b, g_ffn, w_up_b, w_conv, b_conv, w_down_b, g_final, conv_prev)


def _rotary_tables(pos):
    half = HEAD_W // 2
    inv_freq = ROPE_BASE ** (-jnp.arange(half, dtype=F32) / half)
    ang = pos.astype(F32)[:, None] * inv_freq[None, :]
    cos = jnp.cos(ang)
    sin = jnp.sin(ang)
    return jnp.concatenate([cos, cos], axis=-1), jnp.concatenate([-sin, sin], axis=-1)


def kernel(x_prompt, x_sample, cache_k, cache_v, state_ret, state_conv, c_prompt, c_sample,
           w_ada, b_ada, g_mix, w_in, lambda_q1, lambda_k1, lambda_q2, lambda_k2,
           g_sub_a, g_sub_r, w_out, g_ffn, w_up, w_conv, b_conv, w_down, rel_bias, g_final):
    depth = w_in.shape[0]
    assert depth == 1
    l = 0
    nb_p, s_p, _ = x_prompt.shape
    nb_s, s_new, _ = x_sample.shape
    past = cache_k.shape[2]
    n_tok_s = nb_s * s_new

    log_gamma = jnp.log(1.0 - 2.0 ** (-5.0 - jnp.arange(R_HEADS, dtype=F32)))
    lam_init = 0.8 - 0.6 * math.exp(-0.3 * l)
    lam = (jnp.exp(jnp.sum(lambda_q1[l].astype(F32) * lambda_k1[l].astype(F32)))
           - jnp.exp(jnp.sum(lambda_q2[l].astype(F32) * lambda_k2[l].astype(F32))) + lam_init).reshape(1)

    w_in_b = w_in[l].astype(BF16)
    w_out_b = w_out[l].astype(BF16)
    w_up_b = w_up[l].astype(BF16)
    w_down_b = w_down[l].astype(BF16)
    g_mix_l = g_mix[l].reshape(1, D_MODEL)
    g_ffn_l = g_ffn[l].reshape(1, D_MODEL)
    g_fin = g_final.reshape(1, D_MODEL)
    g_sa_row = g_sub_a[l].reshape(1, HEAD_W)
    g_sa_col = g_sub_a[l].reshape(HEAD_W, 1)
    g_sr = g_sub_r[l].reshape(1, HEAD_W)
    w_conv_l = w_conv[l]
    b_conv_l = b_conv[l].reshape(1, 2 * D_FF)

    mods = _adaln(jnp.concatenate([c_prompt, c_sample], axis=0), w_ada[l], b_ada[l])
    mods = mods.reshape(nb_p + nb_s, 6, D_MODEL)
    mod_p = mods[:nb_p].reshape(nb_p, 6, 1, D_MODEL)
    mod_s = jnp.transpose(mods[nb_p:], (1, 0, 2))
    mod_s_stream = jnp.repeat(mod_s, s_new, axis=1)[None]
    mod_s_time = jnp.tile(mod_s, (1, s_new, 1))[None]

    t_attn = 256
    cos_p, sin_p = _rotary_tables(jnp.arange(s_p, dtype=jnp.int32))
    (k32_p, v32_p, qa_p, ka_p, _, qr_p, kr_p, vr_p, gr_p, vt_p) = _proj(
        x_prompt, mod_p, g_mix_l, w_in_b, cos_p, sin_p, tm=512, with_vt=True)

    assert 2 * t_attn - 1 >= MAX_DISTANCE and t_attn % CHUNK == 0
    lanes = jnp.arange(2 * t_attn, dtype=jnp.int32)
    w_p = _bias_vector(rel_bias, jnp.concatenate([t_attn - lanes, -lanes])).reshape(A_HEADS, 2, 2 * t_attn)
    bias_tiles = _bias_prompt(w_p, t_attn)
    oa_p = _attn_prompt(lam, qa_p, ka_p, vt_p, bias_tiles, g_sa_col, lam_init, t_attn, group=4, nh=4)

    or_p, ret_p = _retention(log_gamma, qr_p, kr_p, vr_p, gr_p,
                             jnp.zeros((nb_p, R_HEADS, HEAD_W, HEAD_W), F32), g_sr, c=256, nb=nb_p)

    y_p, conv_p = _mlp(x_prompt, oa_p, or_p, mod_p, w_out_b, g_ffn_l, w_up_b, w_conv_l, b_conv_l,
                       w_down_b, g_fin, jnp.zeros((nb_p, CONV_W - 1, 2 * D_FF), F32), tm=512, shift=1)

    pos_s = past + jnp.arange(s_new, dtype=jnp.int32)
    cos_s, sin_s = _rotary_tables(pos_s)
    cos_s = jnp.tile(cos_s, (nb_s, 1))
    sin_s = jnp.tile(sin_s, (nb_s, 1))
    (k32_s, v32_s, qa_s, ka_s, va_s, qr_s, kr_s, vr_s, gr_s) = _proj(
        x_sample.reshape(1, n_tok_s, D_MODEL), mod_s_stream, g_mix_l, w_in_b, cos_s, sin_s, tm=n_tok_s)

    def streams(t):
        return t.reshape(nb_s, s_new, GROUP_W)

    near, win = 256, 512
    assert near >= MAX_DISTANCE and near + s_new <= win - s_new and past >= near
    k0 = past - near
    w_s = _bias_vector(rel_bias, k0 + jnp.arange(win, dtype=jnp.int32) - past).reshape(A_HEADS, 1, win)
    bias_s = _bias_sample(w_s, s_new, past, k0)
    bias_c = jnp.concatenate([jnp.broadcast_to(bias_s[:, :, :1], (A_HEADS, s_new, k0)),
                              bias_s[:, :, :near]], axis=-1)
    bias_n = bias_s[:, :, near:near + s_new]
    oa_s = _attn_sample(lam, streams(qa_s), cache_k[l].reshape(nb_s, past * A_HEADS, HEAD_W),
                        cache_v[l].reshape(nb_s, past * A_HEADS, HEAD_W), streams(ka_s), streams(va_s),
                        bias_c, bias_n, g_sa_row, lam_init)

    or_s, ret_s = _retention(log_gamma, streams(qr_s), streams(kr_s), streams(vr_s), streams(gr_s),
                             state_ret[l].astype(F32), g_sr, c=s_new, nb=4)

    def time_major(t):
        return jnp.transpose(t, (1, 0, 2)).reshape(1, n_tok_s, t.shape[-1])

    conv_prev_s = jnp.transpose(state_conv[l], (1, 0, 2)).reshape(1, (CONV_W - 1) * nb_s, 2 * D_FF)
    y_s, conv_s = _mlp(time_major(x_sample), time_major(oa_s), time_major(or_s), mod_s_time,
                       w_out_b, g_ffn_l, w_up_b, w_conv_l, b_conv_l, w_down_b, g_fin, conv_prev_s,
                       tm=n_tok_s, shift=nb_s)
    y_s = jnp.transpose(y_s.reshape(s_new, nb_s, D_MODEL), (1, 0, 2))
    conv_s = jnp.transpose(conv_s.reshape(CONV_W - 1, nb_s, 2 * D_FF), (1, 0, 2))

    def heads(t, n):
        return t.reshape(1, n, -1, A_HEADS, HEAD_W)

    return (y_p, y_s, heads(k32_p, nb_p), heads(v32_p, nb_p), ret_p[None], conv_p[None],
            heads(k32_s, nb_s), heads(v32_s, nb_s), ret_s[None], conv_s[None])
```
